```python
import jax, jax.numpy as jnp
from jax import lax
import numpy as np


D_MODEL = 1024
BATCH = 16
SEQ = 4096
DEPTH = 4

CTX_LEN = 256
GRID_W = 64
HEAD_DIM = 128
D_LRU = D_MODEL
LRU_BLOCK_W = 64
LRU_BLOCKS = D_LRU // LRU_BLOCK_W
LRU_C = 8.0
CONV_W = 4
CONV_LEFT = 2
N_HEADS_W = D_MODEL // HEAD_DIM
N_KV_W = N_HEADS_W // 4
WINDOW = 128
N_HEADS_G = D_MODEL // HEAD_DIM
N_KV_G = N_HEADS_G // 4
Q_BLOCK = 128
ROPE_THETA = 10000.0
EPS = 1e-6
NEG_INF = -1e30
N_BRANCH = 3
BRANCH_W = D_MODEL

QW_W, KVW_W = N_HEADS_W * HEAD_DIM, N_KV_W * HEAD_DIM
QW_G, KVW_G = N_HEADS_G * HEAD_DIM, N_KV_G * HEAD_DIM
IN_WIDTHS = (D_LRU, D_LRU,
             QW_W, KVW_W, KVW_W, QW_W,
             QW_G, KVW_G, KVW_G, QW_G,
             D_MODEL, D_MODEL, D_MODEL)
IN_WIDTH = sum(IN_WIDTHS)
SPLIT_POINTS = tuple(int(v) for v in np.cumsum(IN_WIDTHS)[:-1])

kernel_name = "hybrid_rglru_window_axial_prefix_block"


def rmsnorm(x, g):
    xf = x.astype(jnp.float32)
    y = xf * lax.rsqrt(jnp.mean(xf * xf, axis=-1, keepdims=True) + EPS)
    return y.astype(x.dtype) * g


def axial_rope_tables(S):
    rows = S // GRID_W
    row_id = jnp.repeat(jnp.arange(rows), GRID_W)
    col_id = jnp.tile(jnp.arange(GRID_W), rows)
    P = HEAD_DIM // 4
    inv = ROPE_THETA ** (-jnp.arange(P, dtype=jnp.float32) / P)
    ang = jnp.stack([row_id[:, None] * inv, col_id[:, None] * inv], axis=1)
    return jnp.cos(ang), jnp.sin(ang)


def rope_2d(x, cos, sin):
    B, T, H, hd = x.shape
    xs = x.reshape(B, T, H, 2, 2, hd // 4)
    x0, x1 = xs[..., 0, :], xs[..., 1, :]
    c, s = cos[None, :, None], sin[None, :, None]
    return jnp.stack([x0 * c - x1 * s, x1 * c + x0 * s], axis=-2).reshape(B, T, H, hd).astype(x.dtype)


def dwconv(u, w, b):
    T = u.shape[1]
    up = jnp.pad(u, ((0, 0), (CONV_LEFT, CONV_W - 1 - CONV_LEFT), (0, 0)))
    return sum(up[:, k:k + T] * w[k] for k in range(CONV_W)) + b


def lru_coeffs(u, wa, ba, wx, bx, lam):
    ub = u.reshape(*u.shape[:-1], LRU_BLOCKS, LRU_BLOCK_W)
    r = jax.nn.sigmoid(jnp.einsum('btnd,nde->btne', ub, wa).reshape(u.shape) + ba)
    i = jax.nn.sigmoid(jnp.einsum('btnd,nde->btne', ub, wx).reshape(u.shape) + bx)
    log_a = -LRU_C * r.astype(jnp.float32) * jax.nn.softplus(-lam.astype(jnp.float32))
    a = jnp.exp(log_a)
    b = jnp.sqrt(-jnp.expm1(2.0 * log_a)) * (i * u).astype(jnp.float32)
    return a, b


def _combine(left, right):
    a1, b1 = left
    a2, b2 = right
    return a1 * a2, a2 * b1 + b2


def linear_scan(a, b, reverse):
    return lax.associative_scan(_combine, (a, b), reverse=reverse, axis=1)[1]


def rglru_branch(u, uc, conv_w, conv_b, wa, ba, wx, bx, lam):
    u = dwconv(u, conv_w, conv_b)
    uc = dwconv(uc, conv_w, conv_b)
    y, yc = 0.0, 0.0
    for d, rev in enumerate((False, True)):
        ac, bc = lru_coeffs(uc, wa[d], ba[d], wx[d], bx[d], lam[d])
        hc = linear_scan(ac, bc, rev)
        a, b = lru_coeffs(u, wa[d], ba[d], wx[d], bx[d], lam[d])
        if rev:
            b = b.at[:, -1].add(a[:, -1] * hc[:, 0])
        else:
            b = b.at[:, 0].add(a[:, 0] * hc[:, -1])
        y = y + linear_scan(a, b, rev)
        yc = yc + hc
    return y.astype(u.dtype), yc.astype(u.dtype)


def attend(q, kv_sets, sink):
    scale = q.shape[-1] ** -0.5
    logits = []
    for k, v, mask in kv_sets:
        s = jnp.einsum('bqkgd,bjkd->bkgqj', q, k).astype(jnp.float32) * scale
        if mask is not None:
            s = jnp.where(mask, s, NEG_INF)
        logits.append(s)
    if sink is not None:
        B, Q, KVH, G, _ = q.shape
        logits.append(jnp.broadcast_to(sink.astype(jnp.float32).reshape(1, KVH, G, 1, 1), (B, KVH, G, Q, 1)))
    p = jax.nn.softmax(jnp.concatenate(logits, axis=-1), axis=-1)
    out, off = 0.0, 0
    for k, v, _ in kv_sets:
        J = k.shape[1]
        out = out + jnp.einsum('bkgqj,bjkd->bqkgd', p[..., off:off + J].astype(v.dtype), v)
        off += J
    return out


def latent_attention(q, k, v, kc, vc, sink, window):
    B, S, H, hd = q.shape
    KVH = k.shape[2]
    nblk = S // Q_BLOCK
    qb = q.reshape(B, nblk, Q_BLOCK, KVH, H // KVH, hd).swapaxes(0, 1)
    if window is None:
        out = lax.map(lambda qn: attend(qn, [(k, v, None), (kc, vc, None)], sink), qb)
    else:
        span = Q_BLOCK + 2 * window
        pad = ((0, 0), (window, window), (0, 0), (0, 0))
        kp, vp = jnp.pad(k, pad), jnp.pad(v, pad)

        def body(args):
            n, qn = args
            start = n * Q_BLOCK
            kn = lax.dynamic_slice_in_dim(kp, start, span, axis=1)
            vn = lax.dynamic_slice_in_dim(vp, start, span, axis=1)
            qpos = start + jnp.arange(Q_BLOCK)
            kpos = start - window + jnp.arange(span)
            mask = (jnp.abs(kpos[None, :] - qpos[:, None]) <= window) & (kpos >= 0)[None, :] & (kpos < S)[None, :]
            return attend(qn, [(kn, vn, mask), (kc, vc, None)], sink)

        out = lax.map(body, (jnp.arange(nblk), qb))
    return out.swapaxes(0, 1).reshape(B, S, H * hd)


def context_attention(q, k, v, sink):
    B, L, H, hd = q.shape
    KVH = k.shape[2]
    o = attend(q.reshape(B, L, KVH, H // KVH, hd), [(k, v, None)], sink)
    return o.reshape(B, L, H * hd)


def heads(t, n):
    return t.reshape(*t.shape[:-1], n, HEAD_DIM)


def merge_branches(ys, gate_paths, merge_logits, w_branch, w_out):
    out = 0.0
    for n in range(N_BRANCH):
        out = out + jax.nn.sigmoid(merge_logits[n]) * ((ys[n] * jax.nn.silu(gate_paths[n])) @ w_branch[n])
    return out @ w_out


def setup_inputs(seed: int = 0) -> dict:
    key = jax.random.key(seed)
    ks = jax.random.split(key, 24)
    nrm = lambda k, shape, s: jax.random.normal(k, shape, jnp.float32) * s
    x = nrm(ks[0], (BATCH, SEQ, D_MODEL), 1.0)
    c = nrm(ks[1], (BATCH, D_MODEL), 1.0)
    ctx = nrm(ks[2], (BATCH, CTX_LEN, D_MODEL), 1.0)
    c_ctx = nrm(ks[3], (D_MODEL,), 1.0)
    norm_g = 1.0 + nrm(ks[4], (DEPTH, D_MODEL), 0.02)
    w_mod = nrm(ks[5], (DEPTH, D_MODEL, 3 * D_MODEL), 0.5 * D_MODEL ** -0.5)
    b_mod = nrm(ks[6], (DEPTH, 3 * D_MODEL), 0.02)
    w_in = nrm(ks[7], (DEPTH, D_MODEL, IN_WIDTH), D_MODEL ** -0.5)
    conv_w = nrm(ks[8], (DEPTH, CONV_W, D_LRU), CONV_W ** -0.5)
    conv_b = nrm(ks[9], (DEPTH, D_LRU), 0.02)
    lru_wa = nrm(ks[10], (DEPTH, 2, LRU_BLOCKS, LRU_BLOCK_W, LRU_BLOCK_W), LRU_BLOCK_W ** -0.5)
    lru_ba = nrm(ks[11], (DEPTH, 2, D_LRU), 0.1)
    lru_wx = nrm(ks[12], (DEPTH, 2, LRU_BLOCKS, LRU_BLOCK_W, LRU_BLOCK_W), LRU_BLOCK_W ** -0.5)
    lru_bx = nrm(ks[13], (DEPTH, 2, D_LRU), 0.1)
    u = jax.random.uniform(ks[14], (DEPTH, 2, D_LRU), jnp.float32, 0.9, 0.999)
    a = u ** (1.0 / LRU_C)
    lru_lambda = jnp.log(a) - jnp.log1p(-a)
    attn_sink = nrm(ks[15], (DEPTH, N_HEADS_W), 0.5)
    q_norm_g = 1.0 + nrm(ks[16], (DEPTH, HEAD_DIM), 0.02)
    k_norm_g = 1.0 + nrm(ks[17], (DEPTH, HEAD_DIM), 0.02)
    w_branch = nrm(ks[18], (DEPTH, N_BRANCH, BRANCH_W, D_MODEL), BRANCH_W ** -0.5)
    w_out = nrm(ks[19], (DEPTH, D_MODEL, D_MODEL), D_MODEL ** -0.5)
    final_g = 1.0 + nrm(ks[20], (D_MODEL,), 0.02)
    return {"x": x, "c": c, "ctx": ctx, "c_ctx": c_ctx, "norm_g": norm_g, "w_mod": w_mod, "b_mod": b_mod,
            "w_in": w_in, "conv_w": conv_w, "conv_b": conv_b, "lru_wa": lru_wa, "lru_ba": lru_ba,
            "lru_wx": lru_wx, "lru_bx": lru_bx, "lru_lambda": lru_lambda, "attn_sink": attn_sink,
            "q_norm_g": q_norm_g, "k_norm_g": k_norm_g, "w_branch": w_branch, "w_out": w_out,
            "final_g": final_g}


def reference(x, c, ctx, c_ctx, norm_g, w_mod, b_mod, w_in, conv_w, conv_b, lru_wa, lru_ba, lru_wx, lru_bx,
              lru_lambda, attn_sink, q_norm_g, k_norm_g, w_branch, w_out, final_g):
    S = x.shape[1]
    cos, sin = axial_rope_tables(S)
    sc, scc = jax.nn.silu(c), jax.nn.silu(c_ctx)
    xc = ctx
    for l in range(DEPTH):
        last = l == DEPTH - 1
        shift, scale, gate = jnp.split(sc @ w_mod[l] + b_mod[l], 3, axis=-1)
        shift_c, scale_c, gate_c = jnp.split(scc @ w_mod[l] + b_mod[l], 3, axis=-1)
        h = rmsnorm(x, norm_g[l]) * (1.0 + scale[:, None]) + shift[:, None]
        hc = rmsnorm(xc, norm_g[l]) * (1.0 + scale_c) + shift_c
        w_parts = jnp.split(w_in[l], SPLIT_POINTS, axis=1)
        uA, gA, qB, kB, vB, gB, qC, kC, vC, gC, mA, mB, mC = [h @ w for w in w_parts]
        uAc, gAc, qBc, kBc, vBc, gBc, qCc, kCc, vCc, gCc, mAc, mBc, mCc = [hc @ w for w in w_parts]
        yA, yAc = rglru_branch(uA, uAc, conv_w[l], conv_b[l], lru_wa[l], lru_ba[l], lru_wx[l], lru_bx[l], lru_lambda[l])
        kBc_h, vBc_h = heads(kBc, N_KV_W), heads(vBc, N_KV_W)
        yB = latent_attention(rope_2d(heads(qB, N_HEADS_W), cos, sin), rope_2d(heads(kB, N_KV_W), cos, sin),
                              heads(vB, N_KV_W), kBc_h, vBc_h, attn_sink[l], WINDOW)
        kCc_h, vCc_h = rmsnorm(heads(kCc, N_KV_G), k_norm_g[l]), heads(vCc, N_KV_G)
        yC = latent_attention(rope_2d(rmsnorm(heads(qC, N_HEADS_G), q_norm_g[l]), cos, sin),
                              rope_2d(rmsnorm(heads(kC, N_KV_G), k_norm_g[l]), cos, sin),
                              heads(vC, N_KV_G), kCc_h, vCc_h, None, None)
        y = merge_branches((yA, yB, yC), (gA, gB, gC), (mA, mB, mC), w_branch[l], w_out[l])
        if not last:
            yBc = context_attention(heads(qBc, N_HEADS_W), kBc_h, vBc_h, attn_sink[l])
            yCc = context_attention(rmsnorm(heads(qCc, N_HEADS_G), q_norm_g[l]), kCc_h, vCc_h, None)
            yc = merge_branches((yAc, yBc, yCc), (gAc, gBc, gCc), (mAc, mBc, mCc), w_branch[l], w_out[l])
            xc = xc + gate_c * yc
        x = x + gate[:, None] * y
    return rmsnorm(x, final_g)
```

```python
import functools
import math

import jax
import jax.numpy as jnp
from jax import lax
from jax.experimental import pallas as pl
from jax.experimental.pallas import tpu as pltpu

D_MODEL = 1024
HEAD_DIM = 128
N_HEADS = D_MODEL // HEAD_DIM
N_KV = N_HEADS // 4
GROUP = N_HEADS // N_KV
KV_W = N_KV * HEAD_DIM
LRU_BLOCK_W = 64
LRU_C = 8.0
CONV_W = 4
CONV_LEFT = 2
WINDOW = 128
GRID_W = 64
ROPE_THETA = 10000.0
EPS = 1e-6
NEG_INF = -1e30
LOG2E = 1.4426950408889634
Q_SCALE = HEAD_DIM ** -0.5 * LOG2E
GATE_CHUNK = 256
VMEM_LIMIT = 56 * 1024 * 1024

F32 = jnp.float32
BF16 = jnp.bfloat16


def _cparams(sem):
    return pltpu.CompilerParams(dimension_semantics=sem, vmem_limit_bytes=VMEM_LIMIT)


def _sigmoid(x):
    return 0.5 * (jnp.tanh(0.5 * x) + 1.0)


def _mod_kernel(cc_ref, w_ref, b_ref, o_ref):
    cc = cc_ref[...]
    s = cc * _sigmoid(cc)
    o_ref[...] = jnp.dot(s, w_ref[...], preferred_element_type=F32,
                         precision=lax.Precision.HIGHEST) + b_ref[...]


def _modulation(cc, w_mod, b_mod):
    depth = w_mod.shape[0]
    rows = cc.shape[0]
    return pl.pallas_call(
        _mod_kernel,
        grid=(depth, 3),
        in_specs=[
            pl.BlockSpec((rows, D_MODEL), lambda l, j: (0, 0)),
            pl.BlockSpec((None, D_MODEL, D_MODEL), lambda l, j: (l, 0, j)),
            pl.BlockSpec((None, 1, D_MODEL), lambda l, j: (l, 0, j)),
        ],
        out_specs=pl.BlockSpec((None, rows, D_MODEL), lambda l, j: (l, 0, j)),
        out_shape=jax.ShapeDtypeStruct((depth, rows, 3 * D_MODEL), F32),
        compiler_params=_cparams(("arbitrary", "arbitrary")),
        name="modulation",
    )(cc, w_mod, b_mod.reshape(depth, 1, 3 * D_MODEL))


def _norm_mod(x, g, mod):
    ms = jnp.mean(x * x, axis=-1, keepdims=True)
    y = x * lax.rsqrt(ms + EPS) * g
    return y * (1.0 + mod[:, D_MODEL:2 * D_MODEL]) + mod[:, :D_MODEL]


def _rope(yh, c, s1, s2):
    return yh * c + pltpu.roll(yh, HEAD_DIM - 32, 1) * s1 + pltpu.roll(yh, 32, 1) * s2


def _head_norm(yh, g):
    return yh * lax.rsqrt(jnp.mean(yh * yh, axis=-1, keepdims=True) + EPS) * g


def _inproj_kernel(specs, x_ref, mod_ref, g_ref, w_ref, c_ref, s1_ref, s2_ref, qg_ref, kg_ref, *out_refs):
    h = _norm_mod(x_ref[...], g_ref[...], mod_ref[...]).astype(BF16)
    off = 0
    for (kind, width), o_ref in zip(specs, out_refs):
        y = jnp.dot(h, w_ref[:, off:off + width], preferred_element_type=F32)
        off += width
        if kind == "plain":
            o_ref[...] = y.astype(o_ref.dtype)
        elif kind == "silu":
            o_ref[...] = (y * _sigmoid(y)).astype(o_ref.dtype)
        elif kind == "sigmoid":
            o_ref[...] = _sigmoid(y).astype(o_ref.dtype)
        else:
            is_q = kind in ("rope_q", "normrope_q")
            norm = kind in ("normrope_q", "normrope_k")
            c, s1, s2 = c_ref[...], s1_ref[...], s2_ref[...]
            for hh in range(width // HEAD_DIM):
                sl = slice(hh * HEAD_DIM, (hh + 1) * HEAD_DIM)
                yh = y[:, sl]
                if norm:
                    yh = _head_norm(yh, qg_ref[...] if is_q else kg_ref[...])
                yh = _rope(yh, c, s1, s2)
                if is_q:
                    yh = yh * Q_SCALE
                o_ref[:, sl] = yh.astype(o_ref.dtype)


def _inproj(X, mod, g, w, tabs, qg, kg, specs, dtypes, dims, tm, name):
    B, S, L = dims
    n = X.shape[0]
    ts = S // tm
    nlat = B * ts
    wtot = w.shape[1]

    def mod_idx(i):
        return (jnp.where(i < nlat, i // ts, B), 0, 0)

    def tab_idx(i):
        return (jnp.where(i < nlat, i % ts, ts), 0)

    out_shape = [jax.ShapeDtypeStruct((n, wd), dt) for (_, wd), dt in zip(specs, dtypes)]
    out_specs = [pl.BlockSpec((tm, wd), lambda i: (i, 0)) for (_, wd) in specs]
    tab_spec = pl.BlockSpec((tm, HEAD_DIM), tab_idx)
    return pl.pallas_call(
        functools.partial(_inproj_kernel, specs),
        grid=(n // tm,),
        in_specs=[
            pl.BlockSpec((tm, D_MODEL), lambda i: (i, 0)),
            pl.BlockSpec((None, 1, 3 * D_MODEL), mod_idx),
            pl.BlockSpec((1, D_MODEL), lambda i: (0, 0)),
            pl.BlockSpec((D_MODEL, wtot), lambda i: (0, 0)),
            tab_spec, tab_spec, tab_spec,
            pl.BlockSpec((1, HEAD_DIM), lambda i: (0, 0)),
            pl.BlockSpec((1, HEAD_DIM), lambda i: (0, 0)),
        ],
        out_specs=out_specs,
        out_shape=out_shape,
        compiler_params=_cparams(("parallel",)),
        name=name,
    )(X, mod, g, w, tabs[0], tabs[1], tabs[2], qg, kg)


def _lru_kernel(reverse, nchunk, u_ref, up_ref, un_ref, cw_ref, cb_ref, wg_ref, ba_ref, bx_ref, lam_ref,
                y_ref, ext_s, a_s, b_s, h_s):
    T = u_ref.shape[0]
    c = pl.program_id(1)
    j = (nchunk - c) if reverse else (c - 1)
    is_lat = c > 0
    has_prev = jnp.logical_and(is_lat, j > 0)
    has_next = jnp.logical_and(is_lat, j < nchunk - 1)

    ext_s[0:8, :] = jnp.where(has_prev, up_ref[...], 0.0)
    ext_s[8:8 + T, :] = u_ref[...]
    ext_s[8 + T:16 + T, :] = jnp.where(has_next, un_ref[...], 0.0)
    cw = cw_ref[...]
    u = cb_ref[...]
    for k in range(CONV_W):
        o = 8 - CONV_LEFT + k
        u = u + ext_s[o:o + T, :] * cw[k:k + 1, :]

    lam = lam_ref[...]
    neg = -lam
    softplus = jnp.maximum(neg, 0.0) + jnp.log1p(jnp.exp(-jnp.abs(neg)))
    clam = -LRU_C * softplus
    ub = u.astype(BF16)
    for q in range(D_MODEL // GATE_CHUNK):
        sl = slice(q * GATE_CHUNK, (q + 1) * GATE_CHUNK)
        z = jnp.dot(ub[:, sl], wg_ref[q], preferred_element_type=F32)
        r = _sigmoid(z[:, :GATE_CHUNK] + ba_ref[:, sl])
        i = _sigmoid(z[:, GATE_CHUNK:] + bx_ref[:, sl])
        a = jnp.exp(clam[:, sl] * r)
        a_s[:, sl] = a
        b_s[:, sl] = jnp.sqrt(1.0 - a * a) * (i * u[:, sl])

    @pl.when(c == 0)
    def _():
        h_s[...] = jnp.zeros_like(h_s)

    def body(t, h):
        tt = (T - 1 - t) if reverse else t
        h = a_s[pl.ds(tt, 1), :] * h + b_s[pl.ds(tt, 1), :]
        y_ref[pl.ds(tt, 1), :] = h
        return h

    h_s[...] = lax.fori_loop(0, T, body, h_s[...], unroll=8)


def _lru(uA, cw, cb, wg, ba, bx, lam, dims, reverse, name):
    B, S, L = dims
    T = L
    n = uA.shape[0]
    nchunk = S // T
    t8 = T // 8
    last8 = n // 8 - 1

    def chunk(c):
        return (nchunk - c) if reverse else (c - 1)

    def main_idx(b, c):
        return (jnp.where(c == 0, B * nchunk + b, b * nchunk + chunk(c)), 0)

    def prev_idx(b, c):
        return (jnp.maximum((b * nchunk + chunk(c)) * t8 - 1, 0), 0)

    def next_idx(b, c):
        return (jnp.clip((b * nchunk + chunk(c) + 1) * t8, 0, last8), 0)

    row = lambda b, c: (0, 0)
    return pl.pallas_call(
        functools.partial(_lru_kernel, reverse, nchunk),
        grid=(B, nchunk + 1),
        in_specs=[
            pl.BlockSpec((T, D_MODEL), main_idx),
            pl.BlockSpec((8, D_MODEL), prev_idx),
            pl.BlockSpec((8, D_MODEL), next_idx),
            pl.BlockSpec((CONV_W, D_MODEL), row),
            pl.BlockSpec((1, D_MODEL), row),
            pl.BlockSpec((D_MODEL // GATE_CHUNK, GATE_CHUNK, 2 * GATE_CHUNK), lambda b, c: (0, 0, 0)),
            pl.BlockSpec((1, D_MODEL), row),
            pl.BlockSpec((1, D_MODEL), row),
            pl.BlockSpec((1, D_MODEL), row),
        ],
        out_specs=pl.BlockSpec((T, D_MODEL), main_idx),
        out_shape=jax.ShapeDtypeStruct((n, D_MODEL), F32),
        scratch_shapes=[
            pltpu.VMEM((T + 16, D_MODEL), F32),
            pltpu.VMEM((T, D_MODEL), F32),
            pltpu.VMEM((T, D_MODEL), F32),
            pltpu.VMEM((1, D_MODEL), F32),
        ],
        compiler_params=_cparams(("arbitrary", "arbitrary")),
        name=name,
    )(uA, uA, uA, cw, cb, wg, ba, bx, lam)


def _gate_weights(wa, wx):
    per = GATE_CHUNK // LRU_BLOCK_W
    nchunks = wa.shape[0] // per
    eye = jnp.eye(per, dtype=wa.dtype)

    def bd(w):
        w4 = w.reshape(nchunks, per, LRU_BLOCK_W, LRU_BLOCK_W)
        return jnp.einsum("jkde,kl->jkdle", w4, eye).reshape(nchunks, GATE_CHUNK, GATE_CHUNK)

    return jnp.concatenate([bd(wa), bd(wx)], axis=-1).astype(BF16)


def _softmax_pv(scores, values, sink):
    m = scores[0].max(axis=-1, keepdims=True)
    for s in scores[1:]:
        m = jnp.maximum(m, s.max(axis=-1, keepdims=True))
    if sink is not None:
        m = jnp.maximum(m, sink)
    l = jnp.exp2(sink - m) if sink is not None else 0.0
    o = 0.0
    for s, v in zip(scores, values):
        p = jnp.exp2(s - m)
        l = l + p.sum(axis=-1, keepdims=True)
        o = o + jnp.dot(p.astype(BF16), v, preferred_element_type=F32)
    return o * (1.0 / l)


def _qk(q, k):
    return lax.dot_general(q, k, (((1,), (1,)), ((), ())), preferred_element_type=F32)


def _stack_heads(q_ref, g):
    return jnp.concatenate(
        [q_ref[:, (g * GROUP + h) * HEAD_DIM:(g * GROUP + h + 1) * HEAD_DIM] for h in range(GROUP)], axis=0)


def _sink_col(sink_ref, g, tq):
    return jnp.concatenate(
        [jnp.full((tq, 1), sink_ref[g * GROUP + h] * LOG2E, F32) for h in range(GROUP)], axis=0)


def _store_heads(o_ref, o, g, tq):
    for h in range(GROUP):
        sl = slice((g * GROUP + h) * HEAD_DIM, (g * GROUP + h + 1) * HEAD_DIM)
        o_ref[:, sl] = o[h * tq:(h + 1) * tq].astype(o_ref.dtype)


def _attn_kernel(window, nctx, S, sink_ref, q_ref, kl_ref, vl_ref, kc_ref, vc_ref, o_ref):
    tq = q_ref.shape[0]
    i = pl.program_id(1)
    use_sink = window is not None

    def run(latent):
        for g in range(N_KV):
            cs = slice(g * HEAD_DIM, (g + 1) * HEAD_DIM)
            qs = _stack_heads(q_ref, g)
            scores = [_qk(qs, kc_ref[:, cs])]
            values = [vc_ref[:, cs]]
            if latent and window is None:
                scores.append(_qk(qs, kl_ref[:, cs]))
                values.append(vl_ref[:, cs])
            elif latent:
                span = tq + 2 * window
                q0 = (i - nctx) * tq
                start = pl.multiple_of(jnp.clip(q0 - window, 0, S - span), HEAD_DIM)
                s = _qk(qs, kl_ref[pl.ds(start, span), cs])
                qpos = q0 + lax.broadcasted_iota(jnp.int32, (tq, span), 0)
                kpos = start + lax.broadcasted_iota(jnp.int32, (tq, span), 1)
                keep = jnp.abs(kpos - qpos) <= window
                keep = jnp.concatenate([keep] * GROUP, axis=0)
                scores.append(jnp.where(keep, s, NEG_INF))
                values.append(vl_ref[pl.ds(start, span), cs])
            sink = _sink_col(sink_ref, g, tq) if use_sink else None
            _store_heads(o_ref, _softmax_pv(scores, values, sink), g, tq)

    if nctx > 0:
        @pl.when(i < nctx)
        def _():
            run(False)

        @pl.when(i >= nctx)
        def _():
            run(True)
    else:
        run(True)


def _attention(q, k, v, sink, dims, window, tq, with_ctx, name):
    B, S, L = dims
    n = q.shape[0]
    nctx = L // tq if with_ctx else 0
    nlat = S // tq
    lat0 = B * S // tq

    def q_idx(b, i):
        if nctx == 0:
            return (b * nlat + i, 0)
        return (jnp.where(i < nctx, lat0 + b * nctx + i, b * nlat + i - nctx), 0)

    lat_spec = pl.BlockSpec((S, KV_W), lambda b, i: (b, 0))
    ctx_spec = pl.BlockSpec((L, KV_W), lambda b, i: (B * S // L + b, 0))
    return pl.pallas_call(
        functools.partial(_attn_kernel, window, nctx, S),
        grid=(B, nctx + nlat),
        in_specs=[
            pl.BlockSpec(memory_space=pltpu.SMEM),
            pl.BlockSpec((tq, D_MODEL), q_idx),
            lat_spec, lat_spec, ctx_spec, ctx_spec,
        ],
        out_specs=pl.BlockSpec((tq, D_MODEL), q_idx),
        out_shape=jax.ShapeDtypeStruct((n, D_MODEL), BF16),
        compiler_params=_cparams(("parallel", "arbitrary")),
        name=name,
    )(sink, q, k, v, k, v)


def _merge_kernel(final, x_ref, mod_ref, yf_ref, yr_ref, yb_ref, yc_ref, ga_ref, gb_ref, gc_ref, m_ref,
                  wb_ref, wo_ref, fg_ref, o_ref):
    ys = (yf_ref[...] + yr_ref[...], yb_ref[...].astype(F32), yc_ref[...].astype(F32))
    gs = (ga_ref, gb_ref, gc_ref)
    acc = 0.0
    for n in range(3):
        t = (ys[n] * gs[n][...].astype(F32)).astype(BF16)
        br = jnp.dot(t, wb_ref[n], preferred_element_type=F32)
        acc = acc + m_ref[:, n * D_MODEL:(n + 1) * D_MODEL].astype(F32) * br
    y = jnp.dot(acc.astype(BF16), wo_ref[...], preferred_element_type=F32)
    xn = x_ref[...] + mod_ref[:, 2 * D_MODEL:] * y
    if final:
        xn = xn * lax.rsqrt(jnp.mean(xn * xn, axis=-1, keepdims=True) + EPS) * fg_ref[...]
    o_ref[...] = xn


def _merge(X, mod, yf, yr, yb, yc, ga, gb, gc, m, wb, wo, fg, dims, tm, final):
    B, S, L = dims
    n = B * S if final else X.shape[0]
    ts = S // tm
    nlat = B * ts

    def mod_idx(i):
        return (jnp.where(i < nlat, i // ts, B), 0, 0)

    rows = lambda wd: pl.BlockSpec((tm, wd), lambda i: (i, 0))
    return pl.pallas_call(
        functools.partial(_merge_kernel, final),
        grid=(n // tm,),
        in_specs=[
            rows(D_MODEL),
            pl.BlockSpec((None, 1, 3 * D_MODEL), mod_idx),
            rows(D_MODEL), rows(D_MODEL), rows(D_MODEL), rows(D_MODEL),
            rows(D_MODEL), rows(D_MODEL), rows(D_MODEL), rows(3 * D_MODEL),
            pl.BlockSpec((3, D_MODEL, D_MODEL), lambda i: (0, 0, 0)),
            pl.BlockSpec((D_MODEL, D_MODEL), lambda i: (0, 0)),
            pl.BlockSpec((1, D_MODEL), lambda i: (0, 0)),
        ],
        out_specs=rows(D_MODEL),
        out_shape=jax.ShapeDtypeStruct((n, D_MODEL), F32),
        compiler_params=_cparams(("parallel",)),
        name="merge_final" if final else "merge",
    )(X, mod, yf, yr, yb, yc, ga, gb, gc, m, wb, wo, fg)


def _rope_tables(S, tm):
    t = jnp.arange(S)
    P = HEAD_DIM // 4
    inv = ROPE_THETA ** (-jnp.arange(P, dtype=F32) / P)
    ang = jnp.stack([(t // GRID_W)[:, None] * inv, (t % GRID_W)[:, None] * inv], axis=1)
    cos, sin = jnp.cos(ang), jnp.sin(ang)
    zero = jnp.zeros_like(sin)
    c = jnp.stack([cos, cos], axis=2).reshape(S, HEAD_DIM)
    s1 = jnp.stack([-sin, zero], axis=2).reshape(S, HEAD_DIM)
    s2 = jnp.stack([zero, sin], axis=2).reshape(S, HEAD_DIM)
    ident = jnp.zeros((tm, HEAD_DIM), F32)
    return (jnp.concatenate([c, ident + 1.0]), jnp.concatenate([s1, ident]), jnp.concatenate([s2, ident]))


def kernel(x, c, ctx, c_ctx, norm_g, w_mod, b_mod, w_in, conv_w, conv_b, lru_wa, lru_ba, lru_wx, lru_bx,
           lru_lambda, attn_sink, q_norm_g, k_norm_g, w_branch, w_out, final_g):
    B, S, _ = x.shape
    L = ctx.shape[1]
    depth = w_in.shape[0]
    dims = (B, S, L)
    assert S % L == 0 and L % 128 == 0 and S % GRID_W == 0 and S >= 256 + 2 * WINDOW
    tm = math.gcd(1024, math.gcd(S, B * L))
    tm_merge = math.gcd(256, tm)
    tq_w = 256
    tq_g = 128

    X = jnp.concatenate([x.reshape(B * S, D_MODEL), ctx.reshape(B * L, D_MODEL)], axis=0)
    rows = -(-(B + 1) // 8) * 8
    cc = jnp.concatenate([c, c_ctx[None], jnp.zeros((rows - B - 1, D_MODEL), F32)], axis=0)
    mod_all = _modulation(cc, w_mod, b_mod).reshape(depth, rows, 1, 3 * D_MODEL)
    tabs = _rope_tables(S, tm)
    w_in16 = w_in.astype(BF16)
    wb16 = w_branch.astype(BF16)
    wo16 = w_out.astype(BF16)
    row = lambda a: a.reshape(1, -1)

    a_specs = (("plain", D_MODEL), ("silu", D_MODEL))
    b_specs = (("rope_q", D_MODEL), ("rope_k", KV_W), ("plain", KV_W), ("silu", D_MODEL))
    c_specs = (("normrope_q", D_MODEL), ("normrope_k", KV_W), ("plain", KV_W), ("silu", D_MODEL))
    m_specs = (("sigmoid", 3 * D_MODEL),)
    a_w = 2 * D_MODEL
    b_w = 2 * D_MODEL + 2 * KV_W

    out = None
    for l in range(depth):
        last = l == depth - 1
        mod = mod_all[l]
        g = row(norm_g[l])
        qg, kg = row(q_norm_g[l]), row(k_norm_g[l])
        w = w_in16[l]
        proj = functools.partial(_inproj, X, mod, g, tabs=tabs, qg=qg, kg=kg, dims=dims, tm=tm)
        uA, gA = proj(w=w[:, :a_w], specs=a_specs, dtypes=(F32, BF16), name="inproj_a")
        qB, kB, vB, gB = proj(w=w[:, a_w:a_w + b_w], specs=b_specs, dtypes=(BF16,) * 4, name="inproj_b")
        qC, kC, vC, gC = proj(w=w[:, a_w + b_w:a_w + 2 * b_w], specs=c_specs, dtypes=(BF16,) * 4,
                              name="inproj_c")
        (m,) = proj(w=w[:, a_w + 2 * b_w:], specs=m_specs, dtypes=(BF16,), name="inproj_m")

        ys = []
        for d, rev in enumerate((False, True)):
            wg = _gate_weights(lru_wa[l, d], lru_wx[l, d])
            ys.append(_lru(uA, conv_w[l], row(conv_b[l]), wg, row(lru_ba[l, d]), row(lru_bx[l, d]),
                           row(lru_lambda[l, d]), dims, rev, "lru_rev" if rev else "lru_fwd"))
        yB = _attention(qB, kB, vB, attn_sink[l], dims, WINDOW, tq_w, not last, "attn_window")
        yC = _attention(qC, kC, vC, attn_sink[l], dims, None, tq_g, not last, "attn_dense")
        res = _merge(X, mod, ys[0], ys[1], yB, yC, gA, gB, gC, m, wb16[l], wo16[l], row(final_g),
                     dims, tm_merge, last)
        if last:
            out = res.reshape(B, S, D_MODEL)
        else:
            X = res
    return out
```

```python
import functools
import math

import jax
import jax.numpy as jnp
from jax import lax
from jax.experimental import pallas as pl
from jax.experimental.pallas import tpu as pltpu

D_MODEL = 1024
HEAD_DIM = 128
N_HEADS = D_MODEL // HEAD_DIM
N_KV = N_HEADS // 4
GROUP = N_HEADS // N_KV
KV_W = N_KV * HEAD_DIM
LRU_BLOCK_W = 64
LRU_C = 8.0
CONV_W = 4
CONV_LEFT = 2
WINDOW = 128
GRID_W = 64
ROPE_THETA = 10000.0
EPS = 1e-6
NEG_INF = -1e30
LOG2E = 1.4426950408889634
Q_SCALE = HEAD_DIM ** -0.5 * LOG2E
GATE_CHUNK = 256
KEY_CHUNK = 512
VMEM_LIMIT = 56 * 1024 * 1024

F32 = jnp.float32
BF16 = jnp.bfloat16


def _cparams(sem):
    return pltpu.CompilerParams(dimension_semantics=sem, vmem_limit_bytes=VMEM_LIMIT)


def _sigmoid(x):
    return 0.5 * (jnp.tanh(0.5 * x) + 1.0)


def _mod_kernel(cc_ref, w_ref, b_ref, o_ref):
    cc = cc_ref[...]
    s = cc * _sigmoid(cc)
    o_ref[...] = jnp.dot(s, w_ref[...], preferred_element_type=F32,
                         precision=lax.Precision.HIGHEST) + b_ref[...]


def _modulation(cc, w_mod, b_mod):
    depth = w_mod.shape[0]
    rows = cc.shape[0]
    return pl.pallas_call(
        _mod_kernel,
        grid=(depth, 3),
        in_specs=[
            pl.BlockSpec((rows, D_MODEL), lambda l, j: (0, 0)),
            pl.BlockSpec((None, D_MODEL, D_MODEL), lambda l, j: (l, 0, j)),
            pl.BlockSpec((None, 1, D_MODEL), lambda l, j: (l, 0, j)),
        ],
        out_specs=pl.BlockSpec((None, rows, D_MODEL), lambda l, j: (l, 0, j)),
        out_shape=jax.ShapeDtypeStruct((depth, rows, 3 * D_MODEL), F32),
        compiler_params=_cparams(("arbitrary", "arbitrary")),
        name="modulation",
    )(cc, w_mod, b_mod.reshape(depth, 1, 3 * D_MODEL))


def _norm_mod(x, g, mod):
    ms = jnp.mean(x * x, axis=-1, keepdims=True)
    y = x * lax.rsqrt(ms + EPS) * g
    return y * (1.0 + mod[:, D_MODEL:2 * D_MODEL]) + mod[:, :D_MODEL]


def _rope(yh, c, s1, s2):
    return yh * c + pltpu.roll(yh, HEAD_DIM - 32, 1) * s1 + pltpu.roll(yh, 32, 1) * s2


def _head_norm(yh, g):
    return yh * lax.rsqrt(jnp.mean(yh * yh, axis=-1, keepdims=True) + EPS) * g


def _inproj_kernel(specs, x_ref, mod_ref, g_ref, w_ref, c_ref, s1_ref, s2_ref, qg_ref, kg_ref, *out_refs):
    h = _norm_mod(x_ref[...], g_ref[...], mod_ref[...]).astype(BF16)
    off = 0
    for (kind, width), o_ref in zip(specs, out_refs):
        y = jnp.dot(h, w_ref[:, off:off + width], preferred_element_type=F32)
        off += width
        if kind == "plain":
            o_ref[...] = y.astype(o_ref.dtype)
        elif kind == "value":
            for hh in range(width // HEAD_DIM):
                o_ref[:, 2 * hh * HEAD_DIM:(2 * hh + 1) * HEAD_DIM] = (
                    y[:, hh * HEAD_DIM:(hh + 1) * HEAD_DIM].astype(o_ref.dtype))
                o_ref[:, (2 * hh + 1) * HEAD_DIM:(2 * hh + 2) * HEAD_DIM] = jnp.ones(
                    (y.shape[0], HEAD_DIM), o_ref.dtype)
        elif kind == "silu":
            o_ref[...] = (y * _sigmoid(y)).astype(o_ref.dtype)
        elif kind == "sigmoid":
            o_ref[...] = _sigmoid(y).astype(o_ref.dtype)
        else:
            is_q = kind in ("rope_q", "normrope_q")
            norm = kind in ("normrope_q", "normrope_k", "normrope_kT")
            c, s1, s2 = c_ref[...], s1_ref[...], s2_ref[...]
            for hh in range(width // HEAD_DIM):
                sl = slice(hh * HEAD_DIM, (hh + 1) * HEAD_DIM)
                yh = y[:, sl]
                if norm:
                    yh = _head_norm(yh, qg_ref[...] if is_q else kg_ref[...])
                yh = _rope(yh, c, s1, s2)
                if is_q:
                    yh = yh * Q_SCALE
                if kind == "normrope_kT":
                    o_ref[sl, :] = yh.T.astype(o_ref.dtype)
                else:
                    o_ref[:, sl] = yh.astype(o_ref.dtype)


def _inproj(X, mod, g, w, tabs, qg, kg, specs, dtypes, dims, tm, name):
    B, S, L = dims
    n = X.shape[0]
    ts = S // tm
    nlat = B * ts
    wtot = w.shape[1]

    def mod_idx(i):
        return (jnp.where(i < nlat, i // ts, B), 0, 0)

    def tab_idx(i):
        return (jnp.where(i < nlat, i % ts, ts), 0)

    out_shape, out_specs = [], []
    for (kind, wd), dt in zip(specs, dtypes):
        if kind == "normrope_kT":
            out_shape.append(jax.ShapeDtypeStruct((wd, n), dt))
            out_specs.append(pl.BlockSpec((wd, tm), lambda i: (0, i)))
        else:
            wd = 2 * wd if kind == "value" else wd
            out_shape.append(jax.ShapeDtypeStruct((n, wd), dt))
            out_specs.append(pl.BlockSpec((tm, wd), lambda i: (i, 0)))
    tab_spec = pl.BlockSpec((tm, HEAD_DIM), tab_idx)
    return pl.pallas_call(
        functools.partial(_inproj_kernel, specs),
        grid=(n // tm,),
        in_specs=[
            pl.BlockSpec((tm, D_MODEL), lambda i: (i, 0)),
            pl.BlockSpec((None, 1, 3 * D_MODEL), mod_idx),
            pl.BlockSpec((1, D_MODEL), lambda i: (0, 0)),
            pl.BlockSpec((D_MODEL, wtot), lambda i: (0, 0)),
            tab_spec, tab_spec, tab_spec,
            pl.BlockSpec((1, HEAD_DIM), lambda i: (0, 0)),
            pl.BlockSpec((1, HEAD_DIM), lambda i: (0, 0)),
        ],
        out_specs=out_specs,
        out_shape=out_shape,
        compiler_params=_cparams(("parallel",)),
        name=name,
    )(X, mod, g, w, tabs[0], tabs[1], tabs[2], qg, kg)


def _lru_kernel(reverse, nchunk, u_ref, up_ref, un_ref, cw_ref, cb_ref, wg_ref, ba_ref, bx_ref, lam_ref,
                y_ref, ext_s, a_s, b_s, h_s):
    T = u_ref.shape[0]
    c = pl.program_id(1)
    j = (nchunk - c) if reverse else (c - 1)
    is_lat = c > 0
    has_prev = jnp.logical_and(is_lat, j > 0)
    has_next = jnp.logical_and(is_lat, j < nchunk - 1)

    ext_s[0:8, :] = jnp.where(has_prev, up_ref[...], 0.0)
    ext_s[8:8 + T, :] = u_ref[...]
    ext_s[8 + T:16 + T, :] = jnp.where(has_next, un_ref[...], 0.0)
    cw = cw_ref[...]
    u = cb_ref[...]
    ext = ext_s[...]
    for k in range(CONV_W):
        shift = (CONV_LEFT - k) % (T + 16)
        tap = ext if shift == 0 else pltpu.roll(ext, shift, 0)
        u = u + tap[8:8 + T, :] * cw[k:k + 1, :]

    lam = lam_ref[...]
    neg = -lam
    softplus = jnp.maximum(neg, 0.0) + jnp.log1p(jnp.exp(-jnp.abs(neg)))
    c1 = (-0.5 * LRU_C * LOG2E) * softplus
    hba = 0.5 * ba_ref[...]
    hbx = 0.5 * bx_ref[...]
    ub = u.astype(BF16)
    hu = 0.5 * u
    for q in range(D_MODEL // GATE_CHUNK):
        sl = slice(q * GATE_CHUNK, (q + 1) * GATE_CHUNK)
        z = jnp.dot(ub[:, sl], wg_ref[q], preferred_element_type=F32)
        tr = jnp.tanh(z[:, :GATE_CHUNK] + hba[:, sl])
        ti = jnp.tanh(z[:, GATE_CHUNK:] + hbx[:, sl])
        a = jnp.exp2(c1[:, sl] * tr + c1[:, sl])
        a_s[:, sl] = a
        b_s[:, sl] = jnp.sqrt(1.0 - a * a) * ((ti + 1.0) * hu[:, sl])

    @pl.when(c == 0)
    def _():
        h_s[...] = jnp.zeros_like(h_s)

    def body(t, h):
        tt = (T - 1 - t) if reverse else t
        h = a_s[pl.ds(tt, 1), :] * h + b_s[pl.ds(tt, 1), :]
        y_ref[pl.ds(tt, 1), :] = h
        return h

    h_s[...] = lax.fori_loop(0, T, body, h_s[...], unroll=8)


def _lru(uA, cw, cb, wg, ba, bx, lam, dims, reverse, name):
    B, S, L = dims
    T = L
    n = uA.shape[0]
    nchunk = S // T
    t8 = T // 8
    last8 = n // 8 - 1

    def chunk(c):
        return (nchunk - c) if reverse else (c - 1)

    def main_idx(b, c):
        return (jnp.where(c == 0, B * nchunk + b, b * nchunk + chunk(c)), 0)

    def prev_idx(b, c):
        return (jnp.maximum((b * nchunk + chunk(c)) * t8 - 1, 0), 0)

    def next_idx(b, c):
        return (jnp.clip((b * nchunk + chunk(c) + 1) * t8, 0, last8), 0)

    row = lambda b, c: (0, 0)
    return pl.pallas_call(
        functools.partial(_lru_kernel, reverse, nchunk),
        grid=(B, nchunk + 1),
        in_specs=[
            pl.BlockSpec((T, D_MODEL), main_idx),
            pl.BlockSpec((8, D_MODEL), prev_idx),
            pl.BlockSpec((8, D_MODEL), next_idx),
            pl.BlockSpec((CONV_W, D_MODEL), row),
            pl.BlockSpec((1, D_MODEL), row),
            pl.BlockSpec((D_MODEL // GATE_CHUNK, GATE_CHUNK, 2 * GATE_CHUNK), lambda b, c: (0, 0, 0)),
            pl.BlockSpec((1, D_MODEL), row),
            pl.BlockSpec((1, D_MODEL), row),
            pl.BlockSpec((1, D_MODEL), row),
        ],
        out_specs=pl.BlockSpec((T, D_MODEL), main_idx),
        out_shape=jax.ShapeDtypeStruct((n, D_MODEL), F32),
        scratch_shapes=[
            pltpu.VMEM((T + 16, D_MODEL), F32),
            pltpu.VMEM((T, D_MODEL), F32),
            pltpu.VMEM((T, D_MODEL), F32),
            pltpu.VMEM((1, D_MODEL), F32),
        ],
        compiler_params=_cparams(("arbitrary", "arbitrary")),
        name=name,
    )(uA, uA, uA, cw, cb, wg, ba, bx, lam)


def _gate_weights(wa, wx):
    per = GATE_CHUNK // LRU_BLOCK_W
    nchunks = wa.shape[0] // per
    eye = jnp.eye(per, dtype=wa.dtype)

    def bd(w):
        w4 = w.reshape(nchunks, per, LRU_BLOCK_W, LRU_BLOCK_W)
        return jnp.einsum("jkde,kl->jkdle", w4, eye).reshape(nchunks, GATE_CHUNK, GATE_CHUNK)

    return (0.5 * jnp.concatenate([bd(wa), bd(wx)], axis=-1)).astype(BF16)


def _lane_fold(x, acc, op):
    for c in range(x.shape[1] // HEAD_DIM):
        blk = x[:, c * HEAD_DIM:(c + 1) * HEAD_DIM]
        acc = blk if acc is None else op(acc, blk)
    return acc


def _softmax_pv(scores, values, sink):
    mrun = None
    for s in scores:
        mrun = _lane_fold(s, mrun, jnp.maximum)
    m = mrun.max(axis=-1, keepdims=True)
    if sink is not None:
        m = jnp.maximum(m, sink)
    o = None
    for s, v in zip(scores, values):
        d = jnp.dot(jnp.exp2(s - m).astype(BF16), v, preferred_element_type=F32)
        o = d if o is None else o + d
    l = o[:, HEAD_DIM:]
    if sink is not None:
        l = l + jnp.exp2(sink - m)
    return o[:, :HEAD_DIM] * (1.0 / l)


def _qk(q, k):
    return lax.dot_general(q, k, (((1,), (1,)), ((), ())), preferred_element_type=F32)


def _stack_heads(q_ref, g):
    return jnp.concatenate(
        [q_ref[:, (g * GROUP + h) * HEAD_DIM:(g * GROUP + h + 1) * HEAD_DIM] for h in range(GROUP)], axis=0)


def _sink_col(sink_ref, g, tq):
    return jnp.concatenate(
        [jnp.full((tq, 1), sink_ref[g * GROUP + h] * LOG2E, F32) for h in range(GROUP)], axis=0)


def _store_heads(o_ref, o, g, tq):
    for h in range(GROUP):
        sl = slice((g * GROUP + h) * HEAD_DIM, (g * GROUP + h + 1) * HEAD_DIM)
        o_ref[:, sl] = o[h * tq:(h + 1) * tq].astype(o_ref.dtype)


def _kv_slices(g):
    return slice(g * HEAD_DIM, (g + 1) * HEAD_DIM), slice(2 * g * HEAD_DIM, 2 * (g + 1) * HEAD_DIM)


def _window_kernel(S, sink_ref, q_ref, kl_ref, vl_ref, kc_ref, vc_ref, o_ref):
    tq = q_ref.shape[0]
    span = tq + 2 * WINDOW
    q0 = pl.program_id(1) * tq
    start = pl.multiple_of(jnp.clip(q0 - WINDOW, 0, S - span), HEAD_DIM)
    qpos = q0 + lax.broadcasted_iota(jnp.int32, (tq, span), 0)
    kpos = start + lax.broadcasted_iota(jnp.int32, (tq, span), 1)
    bias = jnp.where(jnp.abs(kpos - qpos) <= WINDOW, 0.0, NEG_INF)
    bias = jnp.concatenate([bias] * GROUP, axis=0)
    for g in range(N_KV):
        cs, vs = _kv_slices(g)
        qs = _stack_heads(q_ref, g)
        scores = [_qk(qs, kc_ref[:, cs]), _qk(qs, kl_ref[pl.ds(start, span), cs]) + bias]
        values = [vc_ref[:, vs], vl_ref[pl.ds(start, span), vs]]
        _store_heads(o_ref, _softmax_pv(scores, values, _sink_col(sink_ref, g, tq)), g, tq)


def _window_attention(q, k, v, sink, dims, tq, rows):
    B, S, L = dims
    nlat = S // tq
    lat_spec = lambda wd: pl.BlockSpec((S, wd), lambda b, i: (b, 0))
    ctx_spec = lambda wd: pl.BlockSpec((L, wd), lambda b, i: (B * S // L + b, 0))
    q_spec = pl.BlockSpec((tq, D_MODEL), lambda b, i: (b * nlat + i, 0))
    return pl.pallas_call(
        functools.partial(_window_kernel, S),
        grid=(B, nlat),
        in_specs=[pl.BlockSpec(memory_space=pltpu.SMEM), q_spec,
                  lat_spec(KV_W), lat_spec(2 * KV_W), ctx_spec(KV_W), ctx_spec(2 * KV_W)],
        out_specs=q_spec,
        out_shape=jax.ShapeDtypeStruct((rows, D_MODEL), BF16),
        compiler_params=_cparams(("arbitrary", "arbitrary")),
        name="attn_window",
    )(sink, q, k, v, k, v)


def _dense_kernel(nlat, q_ref, kl_ref, vl_ref, kc_ref, vc_ref, o_ref, p_a, p_b, s_scr):
    tq = q_ref.shape[0]
    L = kc_ref.shape[1]
    S = kl_ref.shape[1]
    i = pl.program_id(1)
    bounds = [(0, L)] + [(L + j0, L + j0 + KEY_CHUNK) for j0 in range(0, S, KEY_CHUNK)]

    def step(p_w, p_r):
        rowmax = []
        for g in range(N_KV):
            if p_w is None:
                break
            cs, _ = _kv_slices(g)
            qs = _stack_heads(q_ref, g)
            mrun = None
            for lo, hi in bounds:
                k = kc_ref[cs, :] if lo == 0 else kl_ref[cs, lo - L:hi - L]
                s = jnp.dot(qs, k, preferred_element_type=F32)
                s_scr[g, :, lo:hi] = s
                mrun = _lane_fold(s, mrun, jnp.maximum)
            rowmax.append(mrun.max(axis=-1, keepdims=True))
        for g in range(N_KV):
            _, vs = _kv_slices(g)
            o = None
            for n, (lo, hi) in enumerate(bounds):
                if p_w is not None:
                    p_w[g, :, lo:hi] = jnp.exp2(s_scr[g, :, lo:hi] - rowmax[g]).astype(BF16)
                if p_r is not None:
                    v = vc_ref[:, vs] if lo == 0 else vl_ref[lo - L:hi - L, vs]
                    d = jnp.dot(p_r[g, :, lo:hi], v, preferred_element_type=F32)
                    o = d if o is None else o + d
            if p_r is not None:
                _store_heads(o_ref, o[:, :HEAD_DIM] * (1.0 / o[:, HEAD_DIM:]), g, tq)

    odd = i % 2 == 1
    steady = jnp.logical_and(i > 0, i < nlat)

    @pl.when(i == 0)
    def _():
        step(p_a, None)

    @pl.when(jnp.logical_and(steady, odd))
    def _():
        step(p_b, p_a)

    @pl.when(jnp.logical_and(steady, jnp.logical_not(odd)))
    def _():
        step(p_a, p_b)

    @pl.when(i == nlat)
    def _():
        step(None, p_b)


def _dense_attention(q, kT, v, dims, tq, rows):
    B, S, L = dims
    nlat = S // tq
    assert nlat % 2 == 0
    one = pl.Buffered(1)
    return pl.pallas_call(
        functools.partial(_dense_kernel, nlat),
        grid=(B, nlat + 1),
        in_specs=[
            pl.BlockSpec((tq, D_MODEL), lambda b, i: (b * nlat + jnp.minimum(i, nlat - 1), 0)),
            pl.BlockSpec((KV_W, S), lambda b, i: (0, b), pipeline_mode=one),
            pl.BlockSpec((S, 2 * KV_W), lambda b, i: (b, 0), pipeline_mode=one),
            pl.BlockSpec((KV_W, L), lambda b, i: (0, B * S // L + b), pipeline_mode=one),
            pl.BlockSpec((L, 2 * KV_W), lambda b, i: (B * S // L + b, 0), pipeline_mode=one),
        ],
        out_specs=pl.BlockSpec((tq, D_MODEL), lambda b, i: (b * nlat + jnp.maximum(i - 1, 0), 0)),
        out_shape=jax.ShapeDtypeStruct((rows, D_MODEL), BF16),
        scratch_shapes=[pltpu.VMEM((N_KV, GROUP * tq, L + S), BF16)] * 2
        + [pltpu.VMEM((N_KV, GROUP * tq, L + S), F32)],
        compiler_params=_cparams(("arbitrary", "arbitrary")),
        name="attn_dense",
    )(q, kT, v, kT, v)


def _ctx_kernel(sink_ref, qb_ref, kb_ref, vb_ref, qc_ref, kc_ref, vc_ref, yb_any, yc_any, ob_ref, oc_ref):
    del yb_any, yc_any
    tq = qb_ref.shape[0]
    for g in range(N_KV):
        cs, vs = _kv_slices(g)
        qs = _stack_heads(qb_ref, g)
        ob = _softmax_pv([_qk(qs, kb_ref[:, cs])], [vb_ref[:, vs]], _sink_col(sink_ref, g, tq))
        _store_heads(ob_ref, ob, g, tq)
        qs = _stack_heads(qc_ref, g)
        oc = _softmax_pv([jnp.dot(qs, kc_ref[cs, :], preferred_element_type=F32)], [vc_ref[:, vs]], None)
        _store_heads(oc_ref, oc, g, tq)


def _ctx_attention(sink, qB, kB, vB, qC, kTC, vC, yB, yC, dims):
    B, S, L = dims
    c0 = B * S // L
    rows = lambda wd: pl.BlockSpec((L, wd), lambda b: (c0 + b, 0))
    anyspec = pl.BlockSpec(memory_space=pl.ANY)
    return pl.pallas_call(
        _ctx_kernel,
        grid=(B,),
        in_specs=[pl.BlockSpec(memory_space=pltpu.SMEM),
                  rows(D_MODEL), rows(KV_W), rows(2 * KV_W),
                  rows(D_MODEL), pl.BlockSpec((KV_W, L), lambda b: (0, c0 + b)), rows(2 * KV_W),
                  anyspec, anyspec],
        out_specs=[rows(D_MODEL), rows(D_MODEL)],
        out_shape=[jax.ShapeDtypeStruct(yB.shape, BF16), jax.ShapeDtypeStruct(yC.shape, BF16)],
        input_output_aliases={7: 0, 8: 1},
        compiler_params=_cparams(("arbitrary",)),
        name="attn_ctx",
    )(sink, qB, kB, vB, qC, kTC, vC, yB, yC)


def _merge_kernel(final, x_ref, mod_ref, yf_ref, yr_ref, yb_ref, yc_ref, ga_ref, gb_ref, gc_ref, m_ref,
                  wb_ref, wo_ref, fg_ref, o_ref):
    ys = (yf_ref[...] + yr_ref[...], yb_ref[...].astype(F32), yc_ref[...].astype(F32))
    gs = (ga_ref, gb_ref, gc_ref)
    acc = 0.0
    for n in range(3):
        t = (ys[n] * gs[n][...].astype(F32)).astype(BF16)
        br = jnp.dot(t, wb_ref[n], preferred_element_type=F32)
        acc = acc + m_ref[:, n * D_MODEL:(n + 1) * D_MODEL].astype(F32) * br
    y = jnp.dot(acc.astype(BF16), wo_ref[...], preferred_element_type=F32)
    xn = x_ref[...] + mod_ref[:, 2 * D_MODEL:] * y
    if final:
        xn = xn * lax.rsqrt(jnp.mean(xn * xn, axis=-1, keepdims=True) + EPS) * fg_ref[...]
    o_ref[...] = xn


def _merge(X, mod, yf, yr, yb, yc, ga, gb, gc, m, wb, wo, fg, dims, tm, final):
    B, S, L = dims
    n = B * S if final else X.shape[0]
    ts = S // tm
    nlat = B * ts

    def mod_idx(i):
        return (jnp.where(i < nlat, i // ts, B), 0, 0)

    rows = lambda wd: pl.BlockSpec((tm, wd), lambda i: (i, 0))
    return pl.pallas_call(
        functools.partial(_merge_kernel, final),
        grid=(n // tm,),
        in_specs=[
            rows(D_MODEL),
            pl.BlockSpec((None, 1, 3 * D_MODEL), mod_idx),
            rows(D_MODEL), rows(D_MODEL), rows(D_MODEL), rows(D_MODEL),
            rows(D_MODEL), rows(D_MODEL), rows(D_MODEL), rows(3 * D_MODEL),
            pl.BlockSpec((3, D_MODEL, D_MODEL), lambda i: (0, 0, 0), pipeline_mode=pl.Buffered(1)),
            pl.BlockSpec((D_MODEL, D_MODEL), lambda i: (0, 0), pipeline_mode=pl.Buffered(1)),
            pl.BlockSpec((1, D_MODEL), lambda i: (0, 0)),
        ],
        out_specs=rows(D_MODEL),
        out_shape=jax.ShapeDtypeStruct((n, D_MODEL), F32),
        compiler_params=_cparams(("parallel",)),
        name="merge_final" if final else "merge",
    )(X, mod, yf, yr, yb, yc, ga, gb, gc, m, wb, wo, fg)


def _rope_tables(S, tm):
    t = jnp.arange(S)
    P = HEAD_DIM // 4
    inv = ROPE_THETA ** (-jnp.arange(P, dtype=F32) / P)
    ang = jnp.stack([(t // GRID_W)[:, None] * inv, (t % GRID_W)[:, None] * inv], axis=1)
    cos, sin = jnp.cos(ang), jnp.sin(ang)
    zero = jnp.zeros_like(sin)
    c = jnp.stack([cos, cos], axis=2).reshape(S, HEAD_DIM)
    s1 = jnp.stack([-sin, zero], axis=2).reshape(S, HEAD_DIM)
    s2 = jnp.stack([zero, sin], axis=2).reshape(S, HEAD_DIM)
    ident = jnp.zeros((tm, HEAD_DIM), F32)
    return (jnp.concatenate([c, ident + 1.0]), jnp.concatenate([s1, ident]), jnp.concatenate([s2, ident]))


def kernel(x, c, ctx, c_ctx, norm_g, w_mod, b_mod, w_in, conv_w, conv_b, lru_wa, lru_ba, lru_wx, lru_bx,
           lru_lambda, attn_sink, q_norm_g, k_norm_g, w_branch, w_out, final_g):
    B, S, _ = x.shape
    L = ctx.shape[1]
    depth = w_in.shape[0]
    dims = (B, S, L)
    assert S % L == 0 and L % 128 == 0 and S % GRID_W == 0 and S >= 256 + 2 * WINDOW
    tm = math.gcd(1024, math.gcd(S, B * L))
    tm_merge = math.gcd(512, tm)
    tq_w = 256
    tq_g = 128

    X = jnp.concatenate([x.reshape(B * S, D_MODEL), ctx.reshape(B * L, D_MODEL)], axis=0)
    rows = -(-(B + 1) // 8) * 8
    cc = jnp.concatenate([c, c_ctx[None], jnp.zeros((rows - B - 1, D_MODEL), F32)], axis=0)
    mod_all = _modulation(cc, w_mod, b_mod).reshape(depth, rows, 1, 3 * D_MODEL)
    tabs = _rope_tables(S, tm)
    w_in16 = w_in.astype(BF16)
    wb16 = w_branch.astype(BF16)
    wo16 = w_out.astype(BF16)
    row = lambda a: a.reshape(1, -1)

    a_specs = (("plain", D_MODEL), ("silu", D_MODEL))
    b_specs = (("rope_q", D_MODEL), ("rope_k", KV_W), ("value", KV_W), ("silu", D_MODEL))
    c_specs = (("normrope_q", D_MODEL), ("normrope_kT", KV_W), ("value", KV_W), ("silu", D_MODEL))
    m_specs = (("sigmoid", 3 * D_MODEL),)
    a_w = 2 * D_MODEL
    b_w = 2 * D_MODEL + 2 * KV_W

    out = None
    for l in range(depth):
        last = l == depth - 1
        mod = mod_all[l]
        g = row(norm_g[l])
        qg, kg = row(q_norm_g[l]), row(k_norm_g[l])
        w = w_in16[l]
        proj = functools.partial(_inproj, X, mod, g, tabs=tabs, qg=qg, kg=kg, dims=dims, tm=tm)
        uA, gA = proj(w=w[:, :a_w], specs=a_specs, dtypes=(F32, BF16), name="inproj_a")
        qB, kB, vB, gB = proj(w=w[:, a_w:a_w + b_w], specs=b_specs, dtypes=(BF16,) * 4, name="inproj_b")
        qC, kC, vC, gC = proj(w=w[:, a_w + b_w:a_w + 2 * b_w], specs=c_specs, dtypes=(BF16,) * 4,
                              name="inproj_c")
        (m,) = proj(w=w[:, a_w + 2 * b_w:], specs=m_specs, dtypes=(BF16,), name="inproj_m")

        ys = []
        for d, rev in enumerate((False, True)):
            wg = _gate_weights(lru_wa[l, d], lru_wx[l, d])
            ys.append(_lru(uA, conv_w[l], row(conv_b[l]), wg, row(lru_ba[l, d]), row(lru_bx[l, d]),
                           row(lru_lambda[l, d]), dims, rev, "lru_rev" if rev else "lru_fwd"))
        rows_out = B * S if last else X.shape[0]
        yB = _window_attention(qB, kB, vB, attn_sink[l], dims, tq_w, rows_out)
        yC = _dense_attention(qC, kC, vC, dims, tq_g, rows_out)
        if not last:
            yB, yC = _ctx_attention(attn_sink[l], qB, kB, vB, qC, kC, vC, yB, yC, dims)
        res = _merge(X, mod, ys[0], ys[1], yB, yC, gA, gB, gC, m, wb16[l], wo16[l], row(final_g),
                     dims, tm_merge, last)
        if last:
            out = res.reshape(B, S, D_MODEL)
        else:
            X = res
    return out
```

```python
import functools
import math

import jax
import jax.numpy as jnp
from jax import lax
from jax.experimental import pallas as pl
from jax.experimental.pallas import tpu as pltpu

D_MODEL = 1024
HEAD_DIM = 128
N_HEADS = D_MODEL // HEAD_DIM
N_KV = N_HEADS // 4
GROUP = N_HEADS // N_KV
KV_W = N_KV * HEAD_DIM
LRU_BLOCK_W = 64
LRU_C = 8.0
CONV_W = 4
CONV_LEFT = 2
WINDOW = 128
GRID_W = 64
ROPE_THETA = 10000.0
EPS = 1e-6
NEG_INF = -1e30
LOG2E = 1.4426950408889634
Q_SCALE = HEAD_DIM ** -0.5 * LOG2E
GATE_CHUNK = 256
KEY_CHUNK = 512
VMEM_LIMIT = 56 * 1024 * 1024

F32 = jnp.float32
BF16 = jnp.bfloat16


def _cparams(sem):
    return pltpu.CompilerParams(dimension_semantics=sem, vmem_limit_bytes=VMEM_LIMIT)


def _sigmoid(x):
    return 0.5 * (jnp.tanh(0.5 * x) + 1.0)


def _mod_kernel(cc_ref, w_ref, b_ref, o_ref):
    cc = cc_ref[...]
    s = cc * _sigmoid(cc)
    o_ref[...] = jnp.dot(s, w_ref[...], preferred_element_type=F32,
                         precision=lax.Precision.HIGHEST) + b_ref[...]


def _modulation(cc, w_mod, b_mod):
    depth = w_mod.shape[0]
    rows = cc.shape[0]
    return pl.pallas_call(
        _mod_kernel,
        grid=(depth, 3),
        in_specs=[
            pl.BlockSpec((rows, D_MODEL), lambda l, j: (0, 0)),
            pl.BlockSpec((None, D_MODEL, D_MODEL), lambda l, j: (l, 0, j)),
            pl.BlockSpec((None, 1, D_MODEL), lambda l, j: (l, 0, j)),
        ],
        out_specs=pl.BlockSpec((None, rows, D_MODEL), lambda l, j: (l, 0, j)),
        out_shape=jax.ShapeDtypeStruct((depth, rows, 3 * D_MODEL), F32),
        compiler_params=_cparams(("arbitrary", "arbitrary")),
        name="modulation",
    )(cc, w_mod, b_mod.reshape(depth, 1, 3 * D_MODEL))


def _norm_mod(x, g, mod):
    ms = jnp.mean(x * x, axis=-1, keepdims=True)
    y = x * lax.rsqrt(ms + EPS) * g
    return y * (1.0 + mod[:, D_MODEL:2 * D_MODEL]) + mod[:, :D_MODEL]


def _rope(yh, c, s1, s2):
    return yh * c + pltpu.roll(yh, HEAD_DIM - 32, 1) * s1 + pltpu.roll(yh, 32, 1) * s2


def _head_norm(yh, g):
    return yh * lax.rsqrt(jnp.mean(yh * yh, axis=-1, keepdims=True) + EPS) * g


def _inproj_kernel(specs, x_ref, mod_ref, g_ref, w_ref, c_ref, s1_ref, s2_ref, qg_ref, kg_ref, *out_refs):
    h = _norm_mod(x_ref[...], g_ref[...], mod_ref[...]).astype(BF16)
    off = 0
    for (kind, width), o_ref in zip(specs, out_refs):
        y = jnp.dot(h, w_ref[:, off:off + width], preferred_element_type=F32)
        off += width
        if kind == "plain":
            o_ref[...] = y.astype(o_ref.dtype)
        elif kind == "value":
            for hh in range(width // HEAD_DIM):
                o_ref[:, 2 * hh * HEAD_DIM:(2 * hh + 1) * HEAD_DIM] = (
                    y[:, hh * HEAD_DIM:(hh + 1) * HEAD_DIM].astype(o_ref.dtype))
                o_ref[:, (2 * hh + 1) * HEAD_DIM:(2 * hh + 2) * HEAD_DIM] = jnp.ones(
                    (y.shape[0], HEAD_DIM), o_ref.dtype)
        elif kind == "silu":
            o_ref[...] = (y * _sigmoid(y)).astype(o_ref.dtype)
        elif kind == "sigmoid":
            o_ref[...] = _sigmoid(y).astype(o_ref.dtype)
        else:
            is_q = kind in ("rope_q", "normrope_q")
            norm = kind in ("normrope_q", "normrope_k", "normrope_kT")
            c, s1, s2 = c_ref[...], s1_ref[...], s2_ref[...]
            for hh in range(width // HEAD_DIM):
                sl = slice(hh * HEAD_DIM, (hh + 1) * HEAD_DIM)
                yh = y[:, sl]
                if norm:
                    yh = _head_norm(yh, qg_ref[...] if is_q else kg_ref[...])
                yh = _rope(yh, c, s1, s2)
                if is_q:
                    yh = yh * Q_SCALE
                if kind == "normrope_kT":
                    o_ref[sl, :] = yh.T.astype(o_ref.dtype)
                else:
                    o_ref[:, sl] = yh.astype(o_ref.dtype)


def _inproj(X, mod, g, w, tabs, qg, kg, specs, dtypes, dims, tm, name):
    B, S, L = dims
    n = X.shape[0]
    ts = S // tm
    nlat = B * ts
    wtot = w.shape[1]

    def mod_idx(i):
        return (jnp.where(i < nlat, i // ts, B), 0, 0)

    def tab_idx(i):
        return (jnp.where(i < nlat, i % ts, ts), 0)

    out_shape, out_specs = [], []
    for (kind, wd), dt in zip(specs, dtypes):
        if kind == "normrope_kT":
            out_shape.append(jax.ShapeDtypeStruct((wd, n), dt))
            out_specs.append(pl.BlockSpec((wd, tm), lambda i: (0, i)))
        else:
            wd = 2 * wd if kind == "value" else wd
            out_shape.append(jax.ShapeDtypeStruct((n, wd), dt))
            out_specs.append(pl.BlockSpec((tm, wd), lambda i: (i, 0)))
    tab_spec = pl.BlockSpec((tm, HEAD_DIM), tab_idx)
    return pl.pallas_call(
        functools.partial(_inproj_kernel, specs),
        grid=(n // tm,),
        in_specs=[
            pl.BlockSpec((tm, D_MODEL), lambda i: (i, 0)),
            pl.BlockSpec((None, 1, 3 * D_MODEL), mod_idx),
            pl.BlockSpec((1, D_MODEL), lambda i: (0, 0)),
            pl.BlockSpec((D_MODEL, wtot), lambda i: (0, 0)),
            tab_spec, tab_spec, tab_spec,
            pl.BlockSpec((1, HEAD_DIM), lambda i: (0, 0)),
            pl.BlockSpec((1, HEAD_DIM), lambda i: (0, 0)),
        ],
        out_specs=out_specs,
        out_shape=out_shape,
        compiler_params=_cparams(("parallel",)),
        name=name,
    )(X, mod, g, w, tabs[0], tabs[1], tabs[2], qg, kg)


def _lru_kernel(reverse, nchunk, u_ref, up_ref, un_ref, cw_ref, cb_ref, wg_ref, ba_ref, bx_ref, lam_ref,
                y_ref, ext_s, a_s, b_s, h_s):
    T = u_ref.shape[0]
    c = pl.program_id(1)
    j = (nchunk - c) if reverse else (c - 1)
    is_lat = c > 0
    has_prev = jnp.logical_and(is_lat, j > 0)
    has_next = jnp.logical_and(is_lat, j < nchunk - 1)

    ext_s[0:8, :] = jnp.where(has_prev, up_ref[...], 0.0)
    ext_s[8:8 + T, :] = u_ref[...]
    ext_s[8 + T:16 + T, :] = jnp.where(has_next, un_ref[...], 0.0)
    cw = cw_ref[...]
    u = cb_ref[...]
    ext = ext_s[...]
    for k in range(CONV_W):
        shift = (CONV_LEFT - k) % (T + 16)
        tap = ext if shift == 0 else pltpu.roll(ext, shift, 0)
        u = u + tap[8:8 + T, :] * cw[k:k + 1, :]

    lam = lam_ref[...]
    neg = -lam
    softplus = jnp.maximum(neg, 0.0) + jnp.log1p(jnp.exp(-jnp.abs(neg)))
    c1 = (-0.5 * LRU_C * LOG2E) * softplus
    hba = 0.5 * ba_ref[...]
    hbx = 0.5 * bx_ref[...]
    ub = u.astype(BF16)
    hu = 0.5 * u
    for q in range(D_MODEL // GATE_CHUNK):
        sl = slice(q * GATE_CHUNK, (q + 1) * GATE_CHUNK)
        z = jnp.dot(ub[:, sl], wg_ref[q], preferred_element_type=F32)
        tr = jnp.tanh(z[:, :GATE_CHUNK] + hba[:, sl])
        ti = jnp.tanh(z[:, GATE_CHUNK:] + hbx[:, sl])
        a = jnp.exp2(c1[:, sl] * tr + c1[:, sl])
        a_s[:, sl] = a
        b_s[:, sl] = jnp.sqrt(1.0 - a * a) * ((ti + 1.0) * hu[:, sl])

    @pl.when(c == 0)
    def _():
        h_s[...] = jnp.zeros_like(h_s)

    def body(t, h):
        tt = (T - 1 - t) if reverse else t
        h = a_s[pl.ds(tt, 1), :] * h + b_s[pl.ds(tt, 1), :]
        y_ref[pl.ds(tt, 1), :] = h
        return h

    h_s[...] = lax.fori_loop(0, T, body, h_s[...], unroll=8)


def _lru(uA, cw, cb, wg, ba, bx, lam, dims, reverse, name):
    B, S, L = dims
    T = L
    n = uA.shape[0]
    nchunk = S // T
    t8 = T // 8
    last8 = n // 8 - 1

    def chunk(c):
        return (nchunk - c) if reverse else (c - 1)

    def main_idx(b, c):
        return (jnp.where(c == 0, B * nchunk + b, b * nchunk + chunk(c)), 0)

    def prev_idx(b, c):
        return (jnp.maximum((b * nchunk + chunk(c)) * t8 - 1, 0), 0)

    def next_idx(b, c):
        return (jnp.clip((b * nchunk + chunk(c) + 1) * t8, 0, last8), 0)

    row = lambda b, c: (0, 0)
    return pl.pallas_call(
        functools.partial(_lru_kernel, reverse, nchunk),
        grid=(B, nchunk + 1),
        in_specs=[
            pl.BlockSpec((T, D_MODEL), main_idx),
            pl.BlockSpec((8, D_MODEL), prev_idx),
            pl.BlockSpec((8, D_MODEL), next_idx),
            pl.BlockSpec((CONV_W, D_MODEL), row),
            pl.BlockSpec((1, D_MODEL), row),
            pl.BlockSpec((D_MODEL // GATE_CHUNK, GATE_CHUNK, 2 * GATE_CHUNK), lambda b, c: (0, 0, 0)),
            pl.BlockSpec((1, D_MODEL), row),
            pl.BlockSpec((1, D_MODEL), row),
            pl.BlockSpec((1, D_MODEL), row),
        ],
        out_specs=pl.BlockSpec((T, D_MODEL), main_idx),
        out_shape=jax.ShapeDtypeStruct((n, D_MODEL), F32),
        scratch_shapes=[
            pltpu.VMEM((T + 16, D_MODEL), F32),
            pltpu.VMEM((T, D_MODEL), F32),
            pltpu.VMEM((T, D_MODEL), F32),
            pltpu.VMEM((1, D_MODEL), F32),
        ],
        compiler_params=_cparams(("arbitrary", "arbitrary")),
        name=name,
    )(uA, uA, uA, cw, cb, wg, ba, bx, lam)


def _gate_weights(wa, wx):
    per = GATE_CHUNK // LRU_BLOCK_W
    nchunks = wa.shape[0] // per
    eye = jnp.eye(per, dtype=wa.dtype)

    def bd(w):
        w4 = w.reshape(nchunks, per, LRU_BLOCK_W, LRU_BLOCK_W)
        return jnp.einsum("jkde,kl->jkdle", w4, eye).reshape(nchunks, GATE_CHUNK, GATE_CHUNK)

    return (0.5 * jnp.concatenate([bd(wa), bd(wx)], axis=-1)).astype(BF16)


def _row_shift(mrun):
    mb = mrun.astype(BF16)
    return jnp.broadcast_to(mb.max(axis=-1, keepdims=True), mb.shape).astype(F32)


def _shifted_exp2(s, shift):
    blocks = [jnp.exp2(s[:, c * HEAD_DIM:(c + 1) * HEAD_DIM] - shift) for c in range(s.shape[1] // HEAD_DIM)]
    return jnp.concatenate(blocks, axis=1).astype(BF16)


def _lane_fold(x, acc, op):
    for c in range(x.shape[1] // HEAD_DIM):
        blk = x[:, c * HEAD_DIM:(c + 1) * HEAD_DIM]
        acc = blk if acc is None else op(acc, blk)
    return acc


def _softmax_pv(scores, values, sink):
    mrun = None
    for s in scores:
        mrun = _lane_fold(s, mrun, jnp.maximum)
    m = mrun.max(axis=-1, keepdims=True)
    if sink is not None:
        m = jnp.maximum(m, sink)
    o = None
    for s, v in zip(scores, values):
        d = jnp.dot(jnp.exp2(s - m).astype(BF16), v, preferred_element_type=F32)
        o = d if o is None else o + d
    l = o[:, HEAD_DIM:]
    if sink is not None:
        l = l + jnp.exp2(sink - m)
    return o[:, :HEAD_DIM] * (1.0 / l)


def _qk(q, k):
    return lax.dot_general(q, k, (((1,), (1,)), ((), ())), preferred_element_type=F32)


def _stack_heads(q_ref, g):
    return jnp.concatenate(
        [q_ref[:, (g * GROUP + h) * HEAD_DIM:(g * GROUP + h + 1) * HEAD_DIM] for h in range(GROUP)], axis=0)


def _sink_col(sink_ref, g, tq):
    return jnp.concatenate(
        [jnp.full((tq, 1), sink_ref[g * GROUP + h] * LOG2E, F32) for h in range(GROUP)], axis=0)


def _sink_lanes(sink_ref, g, tq):
    return jnp.concatenate(
        [jnp.full((tq, HEAD_DIM), sink_ref[g * GROUP + h] * LOG2E, F32) for h in range(GROUP)], axis=0)


def _store_heads(o_ref, o, g, tq):
    for h in range(GROUP):
        sl = slice((g * GROUP + h) * HEAD_DIM, (g * GROUP + h + 1) * HEAD_DIM)
        o_ref[:, sl] = o[h * tq:(h + 1) * tq].astype(o_ref.dtype)


def _kv_slices(g):
    return slice(g * HEAD_DIM, (g + 1) * HEAD_DIM), slice(2 * g * HEAD_DIM, 2 * (g + 1) * HEAD_DIM)


def _window_kernel(nlat, S, sink_ref, q_ref, kl_ref, vl_ref, kc_ref, vc_ref, o_ref, s_a, s_b, m_a, m_b):
    tq = q_ref.shape[0]
    L = kc_ref.shape[0]
    span = tq + 2 * WINDOW
    i = pl.program_id(1)

    def band_start(tile):
        return pl.multiple_of(jnp.clip(tile * tq - WINDOW, 0, S - span), HEAD_DIM)

    def step(write, read):
        if write is not None:
            start = band_start(i)
            qpos = i * tq + lax.broadcasted_iota(jnp.int32, (tq, span), 0)
            kpos = start + lax.broadcasted_iota(jnp.int32, (tq, span), 1)
            bias = jnp.where(jnp.abs(kpos - qpos) <= WINDOW, 0.0, NEG_INF)
            bias = jnp.concatenate([bias] * GROUP, axis=0)
            for g in range(N_KV):
                cs, _ = _kv_slices(g)
                qs = _stack_heads(q_ref, g)
                s_ctx = _qk(qs, kc_ref[:, cs])
                s_lat = _qk(qs, kl_ref[pl.ds(start, span), cs]) + bias
                write[0][g, :, :L] = s_ctx
                write[0][g, :, L:] = s_lat
                mrun = _lane_fold(s_lat, _lane_fold(s_ctx, None, jnp.maximum), jnp.maximum)
                write[1][g] = jnp.maximum(_row_shift(mrun), _sink_lanes(sink_ref, g, tq))
        if read is not None:
            start = band_start(i - 1)
            for g in range(N_KV):
                _, vs = _kv_slices(g)
                m = read[1][g]
                o = jnp.dot(_shifted_exp2(read[0][g, :, :L], m), vc_ref[:, vs], preferred_element_type=F32)
                o = o + jnp.dot(_shifted_exp2(read[0][g, :, L:], m), vl_ref[pl.ds(start, span), vs],
                                preferred_element_type=F32)
                l = o[:, HEAD_DIM:] + jnp.exp2(_sink_lanes(sink_ref, g, tq) - m)
                _store_heads(o_ref, o[:, :HEAD_DIM] * (1.0 / l), g, tq)

    a, b = (s_a, m_a), (s_b, m_b)
    odd = i % 2 == 1
    steady = jnp.logical_and(i > 0, i < nlat)

    @pl.when(i == 0)
    def _():
        step(a, None)

    @pl.when(jnp.logical_and(steady, odd))
    def _():
        step(b, a)

    @pl.when(jnp.logical_and(steady, jnp.logical_not(odd)))
    def _():
        step(a, b)

    @pl.when(i == nlat)
    def _():
        step(None, b)


def _window_attention(q, k, v, sink, dims, tq, rows):
    B, S, L = dims
    nlat = S // tq
    assert nlat % 2 == 0
    M = GROUP * tq
    keys = L + tq + 2 * WINDOW
    lat_spec = lambda wd: pl.BlockSpec((S, wd), lambda b, i: (b, 0))
    ctx_spec = lambda wd: pl.BlockSpec((L, wd), lambda b, i: (B * S // L + b, 0))
    return pl.pallas_call(
        functools.partial(_window_kernel, nlat, S),
        grid=(B, nlat + 1),
        in_specs=[pl.BlockSpec(memory_space=pltpu.SMEM),
                  pl.BlockSpec((tq, D_MODEL), lambda b, i: (b * nlat + jnp.minimum(i, nlat - 1), 0)),
                  lat_spec(KV_W), lat_spec(2 * KV_W), ctx_spec(KV_W), ctx_spec(2 * KV_W)],
        out_specs=pl.BlockSpec((tq, D_MODEL), lambda b, i: (b * nlat + jnp.maximum(i - 1, 0), 0)),
        out_shape=jax.ShapeDtypeStruct((rows, D_MODEL), BF16),
        scratch_shapes=[pltpu.VMEM((N_KV, M, keys), F32)] * 2 + [pltpu.VMEM((N_KV, M, HEAD_DIM), F32)] * 2,
        compiler_params=_cparams(("arbitrary", "arbitrary")),
        name="attn_window",
    )(sink, q, k, v, k, v)


def _dense_kernel(nlat, q_ref, kl_ref, vl_ref, kc_ref, vc_ref, o_ref, s_a, s_b, m_a, m_b):
    tq = q_ref.shape[0]
    L = kc_ref.shape[1]
    S = kl_ref.shape[1]
    i = pl.program_id(1)
    bounds = [(0, L)] + [(L + j0, L + j0 + KEY_CHUNK) for j0 in range(0, S, KEY_CHUNK)]

    def step(write, read):
        for g in range(N_KV):
            if write is None:
                break
            cs, _ = _kv_slices(g)
            qs = _stack_heads(q_ref, g)
            mrun = None
            for lo, hi in bounds:
                k = kc_ref[cs, :] if lo == 0 else kl_ref[cs, lo - L:hi - L]
                s = jnp.dot(qs, k, preferred_element_type=F32)
                write[0][g, :, lo:hi] = s
                mrun = _lane_fold(s, mrun, jnp.maximum)
            write[1][g] = _row_shift(mrun)
        for g in range(N_KV):
            if read is None:
                break
            _, vs = _kv_slices(g)
            m = read[1][g]
            o = None
            for lo, hi in bounds:
                v = vc_ref[:, vs] if lo == 0 else vl_ref[lo - L:hi - L, vs]
                d = jnp.dot(_shifted_exp2(read[0][g, :, lo:hi], m), v, preferred_element_type=F32)
                o = d if o is None else o + d
            _store_heads(o_ref, o[:, :HEAD_DIM] * (1.0 / o[:, HEAD_DIM:]), g, tq)

    a, b = (s_a, m_a), (s_b, m_b)
    odd = i % 2 == 1
    steady = jnp.logical_and(i > 0, i < nlat)

    @pl.when(i == 0)
    def _():
        step(a, None)

    @pl.when(jnp.logical_and(steady, odd))
    def _():
        step(b, a)

    @pl.when(jnp.logical_and(steady, jnp.logical_not(odd)))
    def _():
        step(a, b)

    @pl.when(i == nlat)
    def _():
        step(None, b)


def _dense_attention(q, kT, v, dims, tq, rows):
    B, S, L = dims
    nlat = S // tq
    assert nlat % 2 == 0
    one = pl.Buffered(1)
    return pl.pallas_call(
        functools.partial(_dense_kernel, nlat),
        grid=(B, nlat + 1),
        in_specs=[
            pl.BlockSpec((tq, D_MODEL), lambda b, i: (b * nlat + jnp.minimum(i, nlat - 1), 0)),
            pl.BlockSpec((KV_W, S), lambda b, i: (0, b), pipeline_mode=one),
            pl.BlockSpec((S, 2 * KV_W), lambda b, i: (b, 0), pipeline_mode=one),
            pl.BlockSpec((KV_W, L), lambda b, i: (0, B * S // L + b), pipeline_mode=one),
            pl.BlockSpec((L, 2 * KV_W), lambda b, i: (B * S // L + b, 0), pipeline_mode=one),
        ],
        out_specs=pl.BlockSpec((tq, D_MODEL), lambda b, i: (b * nlat + jnp.maximum(i - 1, 0), 0)),
        out_shape=jax.ShapeDtypeStruct((rows, D_MODEL), BF16),
        scratch_shapes=[pltpu.VMEM((N_KV, GROUP * tq, L + S), F32)] * 2
        + [pltpu.VMEM((N_KV, GROUP * tq, HEAD_DIM), F32)] * 2,
        compiler_params=_cparams(("arbitrary", "arbitrary")),
        name="attn_dense",
    )(q, kT, v, kT, v)


def _ctx_kernel(sink_ref, qb_ref, kb_ref, vb_ref, qc_ref, kc_ref, vc_ref, yb_any, yc_any, ob_ref, oc_ref):
    del yb_any, yc_any
    tq = qb_ref.shape[0]
    for g in range(N_KV):
        cs, vs = _kv_slices(g)
        qs = _stack_heads(qb_ref, g)
        ob = _softmax_pv([_qk(qs, kb_ref[:, cs])], [vb_ref[:, vs]], _sink_col(sink_ref, g, tq))
        _store_heads(ob_ref, ob, g, tq)
        qs = _stack_heads(qc_ref, g)
        oc = _softmax_pv([jnp.dot(qs, kc_ref[cs, :], preferred_element_type=F32)], [vc_ref[:, vs]], None)
        _store_heads(oc_ref, oc, g, tq)


def _ctx_attention(sink, qB, kB, vB, qC, kTC, vC, yB, yC, dims):
    B, S, L = dims
    c0 = B * S // L
    rows = lambda wd: pl.BlockSpec((L, wd), lambda b: (c0 + b, 0))
    anyspec = pl.BlockSpec(memory_space=pl.ANY)
    return pl.pallas_call(
        _ctx_kernel,
        grid=(B,),
        in_specs=[pl.BlockSpec(memory_space=pltpu.SMEM),
                  rows(D_MODEL), rows(KV_W), rows(2 * KV_W),
                  rows(D_MODEL), pl.BlockSpec((KV_W, L), lambda b: (0, c0 + b)), rows(2 * KV_W),
                  anyspec, anyspec],
        out_specs=[rows(D_MODEL), rows(D_MODEL)],
        out_shape=[jax.ShapeDtypeStruct(yB.shape, BF16), jax.ShapeDtypeStruct(yC.shape, BF16)],
        input_output_aliases={7: 0, 8: 1},
        compiler_params=_cparams(("arbitrary",)),
        name="attn_ctx",
    )(sink, qB, kB, vB, qC, kTC, vC, yB, yC)


def _merge_kernel(final, x_ref, mod_ref, yf_ref, yr_ref, yb_ref, yc_ref, ga_ref, gb_ref, gc_ref, m_ref,
                  wb_ref, wo_ref, fg_ref, o_ref):
    ys = (yf_ref[...] + yr_ref[...], yb_ref[...].astype(F32), yc_ref[...].astype(F32))
    gs = (ga_ref, gb_ref, gc_ref)
    acc = 0.0
    for n in range(3):
        t = (ys[n] * gs[n][...].astype(F32)).astype(BF16)
        br = jnp.dot(t, wb_ref[n], preferred_element_type=F32)
        acc = acc + m_ref[:, n * D_MODEL:(n + 1) * D_MODEL].astype(F32) * br
    y = jnp.dot(acc.astype(BF16), wo_ref[...], preferred_element_type=F32)
    xn = x_ref[...] + mod_ref[:, 2 * D_MODEL:] * y
    if final:
        xn = xn * lax.rsqrt(jnp.mean(xn * xn, axis=-1, keepdims=True) + EPS) * fg_ref[...]
    o_ref[...] = xn


def _merge(X, mod, yf, yr, yb, yc, ga, gb, gc, m, wb, wo, fg, dims, tm, final):
    B, S, L = dims
    n = B * S if final else X.shape[0]
    ts = S // tm
    nlat = B * ts

    def mod_idx(i):
        return (jnp.where(i < nlat, i // ts, B), 0, 0)

    rows = lambda wd: pl.BlockSpec((tm, wd), lambda i: (i, 0))
    return pl.pallas_call(
        functools.partial(_merge_kernel, final),
        grid=(n // tm,),
        in_specs=[
            rows(D_MODEL),
            pl.BlockSpec((None, 1, 3 * D_MODEL), mod_idx),
            rows(D_MODEL), rows(D_MODEL), rows(D_MODEL), rows(D_MODEL),
            rows(D_MODEL), rows(D_MODEL), rows(D_MODEL), rows(3 * D_MODEL),
            pl.BlockSpec((3, D_MODEL, D_MODEL), lambda i: (0, 0, 0), pipeline_mode=pl.Buffered(1)),
            pl.BlockSpec((D_MODEL, D_MODEL), lambda i: (0, 0), pipeline_mode=pl.Buffered(1)),
            pl.BlockSpec((1, D_MODEL), lambda i: (0, 0)),
        ],
        out_specs=rows(D_MODEL),
        out_shape=jax.ShapeDtypeStruct((n, D_MODEL), F32),
        compiler_params=_cparams(("parallel",)),
        name="merge_final" if final else "merge",
    )(X, mod, yf, yr, yb, yc, ga, gb, gc, m, wb, wo, fg)


def _rope_tables(S, tm):
    t = jnp.arange(S)
    P = HEAD_DIM // 4
    inv = ROPE_THETA ** (-jnp.arange(P, dtype=F32) / P)
    ang = jnp.stack([(t // GRID_W)[:, None] * inv, (t % GRID_W)[:, None] * inv], axis=1)
    cos, sin = jnp.cos(ang), jnp.sin(ang)
    zero = jnp.zeros_like(sin)
    c = jnp.stack([cos, cos], axis=2).reshape(S, HEAD_DIM)
    s1 = jnp.stack([-sin, zero], axis=2).reshape(S, HEAD_DIM)
    s2 = jnp.stack([zero, sin], axis=2).reshape(S, HEAD_DIM)
    ident = jnp.zeros((tm, HEAD_DIM), F32)
    return (jnp.concatenate([c, ident + 1.0]), jnp.concatenate([s1, ident]), jnp.concatenate([s2, ident]))


def kernel(x, c, ctx, c_ctx, norm_g, w_mod, b_mod, w_in, conv_w, conv_b, lru_wa, lru_ba, lru_wx, lru_bx,
           lru_lambda, attn_sink, q_norm_g, k_norm_g, w_branch, w_out, final_g):
    B, S, _ = x.shape
    L = ctx.shape[1]
    depth = w_in.shape[0]
    dims = (B, S, L)
    assert S % L == 0 and L % 128 == 0 and S % GRID_W == 0 and S >= 256 + 2 * WINDOW
    tm = math.gcd(1024, math.gcd(S, B * L))
    tm_merge = math.gcd(512, tm)
    tq_w = 256
    tq_g = 128

    X = jnp.concatenate([x.reshape(B * S, D_MODEL), ctx.reshape(B * L, D_MODEL)], axis=0)
    rows = -(-(B + 1) // 8) * 8
    cc = jnp.concatenate([c, c_ctx[None], jnp.zeros((rows - B - 1, D_MODEL), F32)], axis=0)
    mod_all = _modulation(cc, w_mod, b_mod).reshape(depth, rows, 1, 3 * D_MODEL)
    tabs = _rope_tables(S, tm)
    w_in16 = w_in.astype(BF16)
    wb16 = w_branch.astype(BF16)
    wo16 = w_out.astype(BF16)
    row = lambda a: a.reshape(1, -1)

    a_specs = (("plain", D_MODEL), ("silu", D_MODEL))
    b_specs = (("rope_q", D_MODEL), ("rope_k", KV_W), ("value", KV_W), ("silu", D_MODEL))
    c_specs = (("normrope_q", D_MODEL), ("normrope_kT", KV_W), ("value", KV_W), ("silu", D_MODEL))
    m_specs = (("sigmoid", 3 * D_MODEL),)
    a_w = 2 * D_MODEL
    b_w = 2 * D_MODEL + 2 * KV_W

    out = None
    for l in range(depth):
        last = l == depth - 1
        mod = mod_all[l]
        g = row(norm_g[l])
        qg, kg = row(q_norm_g[l]), row(k_norm_g[l])
        w = w_in16[l]
        proj = functools.partial(_inproj, X, mod, g, tabs=tabs, qg=qg, kg=kg, dims=dims, tm=tm)
        uA, gA = proj(w=w[:, :a_w], specs=a_specs, dtypes=(F32, BF16), name="inproj_a")
        qB, kB, vB, gB = proj(w=w[:, a_w:a_w + b_w], specs=b_specs, dtypes=(BF16,) * 4, name="inproj_b")
        qC, kC, vC, gC = proj(w=w[:, a_w + b_w:a_w + 2 * b_w], specs=c_specs, dtypes=(BF16,) * 4,
                              name="inproj_c")
        (m,) = proj(w=w[:, a_w + 2 * b_w:], specs=m_specs, dtypes=(BF16,), name="inproj_m")

        ys = []
        for d, rev in enumerate((False, True)):
            wg = _gate_weights(lru_wa[l, d], lru_wx[l, d])
            ys.append(_lru(uA, conv_w[l], row(conv_b[l]), wg, row(lru_ba[l, d]), row(lru_bx[l, d]),
                           row(lru_lambda[l, d]), dims, rev, "lru_rev" if rev else "lru_fwd"))
        rows_out = B * S if last else X.shape[0]
        yB = _window_attention(qB, kB, vB, attn_sink[l], dims, tq_w, rows_out)
        yC = _dense_attention(qC, kC, vC, dims, tq_g, rows_out)
        if not last:
            yB, yC = _ctx_attention(attn_sink[l], qB, kB, vB, qC, kC, vC, yB, yC, dims)
        res = _merge(X, mod, ys[0], ys[1], yB, yC, gA, gB, gC, m, wb16[l], wo16[l], row(final_g),
                     dims, tm_merge, last)
        if last:
            out = res.reshape(B, S, D_MODEL)
        else:
            X = res
    return out
```

```python
import functools
import math

import jax
import jax.numpy as jnp
from jax import lax
from jax.experimental import pallas as pl
from jax.experimental.pallas import tpu as pltpu

D_MODEL = 1024
HEAD_DIM = 128
N_HEADS = D_MODEL // HEAD_DIM
N_KV = N_HEADS // 4
GROUP = N_HEADS // N_KV
KV_W = N_KV * HEAD_DIM
LRU_BLOCK_W = 64
LRU_C = 8.0
CONV_W = 4
CONV_LEFT = 2
WINDOW = 128
GRID_W = 64
ROPE_THETA = 10000.0
EPS = 1e-6
NEG_INF = -1e30
LOG2E = 1.4426950408889634
Q_SCALE = HEAD_DIM ** -0.5 * LOG2E
GATE_CHUNK = 256
KEY_CHUNK = 512
LRU_BATCH = 4
VMEM_LIMIT = 56 * 1024 * 1024

F32 = jnp.float32
BF16 = jnp.bfloat16


def _cparams(sem):
    return pltpu.CompilerParams(dimension_semantics=sem, vmem_limit_bytes=VMEM_LIMIT)


def _sigmoid(x):
    return 0.5 * (jnp.tanh(0.5 * x) + 1.0)


def _mod_kernel(cc_ref, w_ref, b_ref, o_ref):
    cc = cc_ref[...]
    s = cc * _sigmoid(cc)
    o_ref[...] = jnp.dot(s, w_ref[...], preferred_element_type=F32,
                         precision=lax.Precision.HIGHEST) + b_ref[...]


def _modulation(cc, w_mod, b_mod):
    depth = w_mod.shape[0]
    rows = cc.shape[0]
    return pl.pallas_call(
        _mod_kernel,
        grid=(depth, 3),
        in_specs=[
            pl.BlockSpec((rows, D_MODEL), lambda l, j: (0, 0)),
            pl.BlockSpec((None, D_MODEL, D_MODEL), lambda l, j: (l, 0, j)),
            pl.BlockSpec((None, 1, D_MODEL), lambda l, j: (l, 0, j)),
        ],
        out_specs=pl.BlockSpec((None, rows, D_MODEL), lambda l, j: (l, 0, j)),
        out_shape=jax.ShapeDtypeStruct((depth, rows, 3 * D_MODEL), F32),
        compiler_params=_cparams(("arbitrary", "arbitrary")),
        name="modulation",
    )(cc, w_mod, b_mod.reshape(depth, 1, 3 * D_MODEL))


def _norm_mod(x, g, mod):
    ms = jnp.mean(x * x, axis=-1, keepdims=True)
    y = x * lax.rsqrt(ms + EPS) * g
    return y * (1.0 + mod[:, D_MODEL:2 * D_MODEL]) + mod[:, :D_MODEL]


def _pair_layout(w):
    lead = w.shape[:-1]
    nh = w.shape[-1] // HEAD_DIM
    return w.reshape(*lead, nh, 2, 2, HEAD_DIM // 4).swapaxes(-3, -2).reshape(*lead, nh * HEAD_DIM)


def _rope(yh, c, s):
    return yh * c + pltpu.roll(yh, HEAD_DIM // 2, 1) * s


def _head_norm(yh, g):
    avg = jnp.full((HEAD_DIM, HEAD_DIM), 1.0 / HEAD_DIM, BF16)
    ms = jnp.dot((yh * yh).astype(BF16), avg, preferred_element_type=F32)
    return yh * lax.rsqrt(ms + EPS) * g


def _inproj_kernel(specs, x_ref, mod_ref, g_ref, w_ref, c_ref, s_ref, qg_ref, kg_ref, *out_refs):
    h = _norm_mod(x_ref[...], g_ref[...], mod_ref[...]).astype(BF16)
    off = 0
    for (kind, width), o_ref in zip(specs, out_refs):
        y = jnp.dot(h, w_ref[:, off:off + width], preferred_element_type=F32)
        off += width
        if kind == "plain":
            o_ref[...] = y.astype(o_ref.dtype)
        elif kind == "value":
            for hh in range(width // HEAD_DIM):
                o_ref[:, 2 * hh * HEAD_DIM:(2 * hh + 1) * HEAD_DIM] = (
                    y[:, hh * HEAD_DIM:(hh + 1) * HEAD_DIM].astype(o_ref.dtype))
                o_ref[:, (2 * hh + 1) * HEAD_DIM:(2 * hh + 2) * HEAD_DIM] = jnp.ones(
                    (y.shape[0], HEAD_DIM), o_ref.dtype)
        elif kind == "silu":
            o_ref[...] = (y * _sigmoid(y)).astype(o_ref.dtype)
        elif kind == "sigmoid":
            o_ref[...] = _sigmoid(y).astype(o_ref.dtype)
        else:
            is_q = kind in ("rope_q", "normrope_q")
            norm = kind in ("normrope_q", "normrope_k")
            c, s = c_ref[...], s_ref[...]
            for hh in range(width // HEAD_DIM):
                sl = slice(hh * HEAD_DIM, (hh + 1) * HEAD_DIM)
                yh = y[:, sl]
                if norm:
                    yh = _head_norm(yh, qg_ref[...] if is_q else kg_ref[...])
                yh = _rope(yh, c, s)
                if is_q:
                    yh = yh * Q_SCALE
                o_ref[:, sl] = yh.astype(o_ref.dtype)


def _inproj(X, mod, g, w, tabs, qg, kg, specs, dtypes, dims, tm, name):
    B, S, L = dims
    n = X.shape[0]
    ts = S // tm
    nlat = B * ts
    wtot = w.shape[1]

    def mod_idx(i):
        return (jnp.where(i < nlat, i // ts, B), 0, 0)

    def tab_idx(i):
        return (jnp.where(i < nlat, i % ts, ts), 0)

    widths = [2 * wd if kind == "value" else wd for kind, wd in specs]
    out_shape = [jax.ShapeDtypeStruct((n, wd), dt) for wd, dt in zip(widths, dtypes)]
    out_specs = [pl.BlockSpec((tm, wd), lambda i: (i, 0)) for wd in widths]
    tab_spec = pl.BlockSpec((tm, HEAD_DIM), tab_idx)
    return pl.pallas_call(
        functools.partial(_inproj_kernel, specs),
        grid=(n // tm,),
        in_specs=[
            pl.BlockSpec((tm, D_MODEL), lambda i: (i, 0)),
            pl.BlockSpec((None, 1, 3 * D_MODEL), mod_idx),
            pl.BlockSpec((1, D_MODEL), lambda i: (0, 0)),
            pl.BlockSpec((D_MODEL, wtot), lambda i: (0, 0)),
            tab_spec, tab_spec,
            pl.BlockSpec((1, HEAD_DIM), lambda i: (0, 0)),
            pl.BlockSpec((1, HEAD_DIM), lambda i: (0, 0)),
        ],
        out_specs=out_specs,
        out_shape=out_shape,
        compiler_params=_cparams(("parallel",)),
        name=name,
    )(X, mod, g, w, tabs[0], tabs[1], qg, kg)


def _lru_kernel(reverse, nchunk, u_ref, up_ref, un_ref, h0_ref, cw_ref, cb_ref, wg_ref, ba_ref, bx_ref, lam_ref,
                *rest):
    y_ref, hout_ref, ext_s, a_s, b_s, h_s = rest[-6:]
    K, T, _ = u_ref.shape
    c = pl.program_id(1)
    j = (nchunk - 1 - c) if reverse else c
    has_prev = j > 0
    has_next = j < nchunk - 1

    lam = lam_ref[...]
    neg = -lam
    softplus = jnp.maximum(neg, 0.0) + jnp.log1p(jnp.exp(-jnp.abs(neg)))
    c1 = (-0.5 * LRU_C * LOG2E) * softplus
    hba = 0.5 * ba_ref[...]
    hbx = 0.5 * bx_ref[...]
    cw = cw_ref[...]

    for kb in range(K):
        ext_s[kb, 0:8, :] = jnp.where(has_prev, up_ref[kb], 0.0)
        ext_s[kb, 8:8 + T, :] = u_ref[kb]
        ext_s[kb, 8 + T:16 + T, :] = jnp.where(has_next, un_ref[kb], 0.0)
        ext = ext_s[kb]
        u = cb_ref[...]
        for k in range(CONV_W):
            shift = (CONV_LEFT - k) % (T + 16)
            tap = ext if shift == 0 else pltpu.roll(ext, shift, 0)
            u = u + tap[8:8 + T, :] * cw[k:k + 1, :]
        ub = u.astype(BF16)
        hu = 0.5 * u
        for q in range(D_MODEL // GATE_CHUNK):
            sl = slice(q * GATE_CHUNK, (q + 1) * GATE_CHUNK)
            z = jnp.dot(ub[:, sl], wg_ref[q], preferred_element_type=F32)
            tr = jnp.tanh(z[:, :GATE_CHUNK] + hba[:, sl])
            ti = jnp.tanh(z[:, GATE_CHUNK:] + hbx[:, sl])
            a = jnp.exp2(c1[:, sl] * tr + c1[:, sl])
            a_s[kb, :, sl] = a
            b_s[kb, :, sl] = jnp.sqrt(1.0 - a * a) * ((ti + 1.0) * hu[:, sl])

    @pl.when(c == 0)
    def _():
        h_s[...] = h0_ref[...]

    def body(t, hs):
        tt = (T - 1 - t) if reverse else t
        out = []
        for kb in range(K):
            h = a_s[kb, pl.ds(tt, 1), :] * hs[kb] + b_s[kb, pl.ds(tt, 1), :]
            y_ref[kb, pl.ds(tt, 1), :] = h
            out.append(h)
        return tuple(out)

    hs = lax.fori_loop(0, T, body, tuple(h_s[kb:kb + 1, :] for kb in range(K)), unroll=4)
    for kb in range(K):
        h_s[kb:kb + 1, :] = hs[kb]

    @pl.when(c == nchunk - 1)
    def _():
        hout_ref[...] = h_s[...]


def _lru_call(u3, h0, y_prev, weights, K, T, seg0, reverse, name):
    rows = u3.shape[1]
    nchunk = rows // T
    t8 = T // 8
    blk0 = seg0 // K

    def chunk(c):
        return (nchunk - 1 - c) if reverse else c

    main = pl.BlockSpec((K, T, D_MODEL), lambda g, c: (blk0 + g, chunk(c), 0))
    prev = pl.BlockSpec((K, 8, D_MODEL), lambda g, c: (blk0 + g, jnp.maximum(chunk(c) * t8 - 1, 0), 0))
    nxt = pl.BlockSpec((K, 8, D_MODEL),
                       lambda g, c: (blk0 + g, jnp.minimum((chunk(c) + 1) * t8, rows // 8 - 1), 0))
    hspec = pl.BlockSpec((None, K, D_MODEL), lambda g, c: (g, 0, 0))
    fixed = lambda shape: pl.BlockSpec(shape, lambda g, c: (0,) * len(shape))
    in_specs = [main, prev, nxt, hspec, fixed((CONV_W, D_MODEL)), fixed((1, D_MODEL)),
                fixed((D_MODEL // GATE_CHUNK, GATE_CHUNK, 2 * GATE_CHUNK)),
                fixed((1, D_MODEL)), fixed((1, D_MODEL)), fixed((1, D_MODEL))]
    args = [u3, u3, u3, h0, *weights]
    aliases = {}
    if y_prev is not None:
        in_specs.append(pl.BlockSpec(memory_space=pl.ANY))
        args.append(y_prev)
        aliases = {len(args) - 1: 0}
    return pl.pallas_call(
        functools.partial(_lru_kernel, reverse, nchunk),
        grid=(h0.shape[0], nchunk),
        in_specs=in_specs,
        out_specs=[main, hspec],
        out_shape=[jax.ShapeDtypeStruct(u3.shape, F32), jax.ShapeDtypeStruct(h0.shape, F32)],
        input_output_aliases=aliases,
        scratch_shapes=[
            pltpu.VMEM((K, T + 16, D_MODEL), F32),
            pltpu.VMEM((K, T, D_MODEL), F32),
            pltpu.VMEM((K, T, D_MODEL), F32),
            pltpu.VMEM((K, D_MODEL), F32),
        ],
        compiler_params=_cparams(("arbitrary", "arbitrary")),
        name=name,
    )(*args)


def _lru(uA, weights, dims, reverse, name):
    B, S, L = dims
    n = uA.shape[0]
    K = LRU_BATCH
    assert B % K == 0 and n % S == 0 and (B * S // L) % K == 0
    h0 = jnp.zeros((B // K, K, D_MODEL), F32)
    y, h = _lru_call(uA.reshape(n // L, L, D_MODEL), h0, None, weights, K, L, B * S // L, reverse, name + "_ctx")
    y, _ = _lru_call(uA.reshape(n // S, S, D_MODEL), h, y.reshape(n // S, S, D_MODEL), weights, K, L, 0, reverse,
                     name)
    return y.reshape(n, D_MODEL)


def _gate_weights(wa, wx):
    per = GATE_CHUNK // LRU_BLOCK_W
    nchunks = wa.shape[0] // per
    eye = jnp.eye(per, dtype=wa.dtype)

    def bd(w):
        w4 = w.reshape(nchunks, per, LRU_BLOCK_W, LRU_BLOCK_W)
        return jnp.einsum("jkde,kl->jkdle", w4, eye).reshape(nchunks, GATE_CHUNK, GATE_CHUNK)

    return (0.5 * jnp.concatenate([bd(wa), bd(wx)], axis=-1)).astype(BF16)


def _row_shift(mrun):
    mb = mrun.astype(BF16)
    return jnp.broadcast_to(mb.max(axis=-1, keepdims=True), mb.shape).astype(F32)


def _shifted_exp2(s, shift):
    blocks = [jnp.exp2(s[:, c * HEAD_DIM:(c + 1) * HEAD_DIM] - shift) for c in range(s.shape[1] // HEAD_DIM)]
    return jnp.concatenate(blocks, axis=1).astype(BF16)


def _lane_fold(x, acc, op):
    for c in range(x.shape[1] // HEAD_DIM):
        blk = x[:, c * HEAD_DIM:(c + 1) * HEAD_DIM]
        acc = blk if acc is None else op(acc, blk)
    return acc


def _softmax_pv(scores, values, sink):
    mrun = None
    for s in scores:
        mrun = _lane_fold(s, mrun, jnp.maximum)
    m = mrun.max(axis=-1, keepdims=True)
    if sink is not None:
        m = jnp.maximum(m, sink)
    o = None
    for s, v in zip(scores, values):
        d = jnp.dot(jnp.exp2(s - m).astype(BF16), v, preferred_element_type=F32)
        o = d if o is None else o + d
    l = o[:, HEAD_DIM:]
    if sink is not None:
        l = l + jnp.exp2(sink - m)
    return o[:, :HEAD_DIM] * (1.0 / l)


def _qk(q, k):
    return lax.dot_general(q, k, (((1,), (1,)), ((), ())), preferred_element_type=F32)


def _stack_heads(q_ref, g):
    return jnp.concatenate(
        [q_ref[:, (g * GROUP + h) * HEAD_DIM:(g * GROUP + h + 1) * HEAD_DIM] for h in range(GROUP)], axis=0)


def _sink_col(sink_ref, g, tq):
    return jnp.concatenate(
        [jnp.full((tq, 1), sink_ref[g * GROUP + h] * LOG2E, F32) for h in range(GROUP)], axis=0)


def _sink_lanes(sink_ref, g, tq):
    return jnp.concatenate(
        [jnp.full((tq, HEAD_DIM), sink_ref[g * GROUP + h] * LOG2E, F32) for h in range(GROUP)], axis=0)


def _store_heads(o_ref, o, g, tq):
    for h in range(GROUP):
        sl = slice((g * GROUP + h) * HEAD_DIM, (g * GROUP + h + 1) * HEAD_DIM)
        o_ref[:, sl] = o[h * tq:(h + 1) * tq].astype(o_ref.dtype)


def _kv_slices(g):
    return slice(g * HEAD_DIM, (g + 1) * HEAD_DIM), slice(2 * g * HEAD_DIM, 2 * (g + 1) * HEAD_DIM)


def _window_kernel(nlat, S, sink_ref, q_ref, kl_ref, vl_ref, kc_ref, vc_ref, o_ref, s_a, s_b, m_a, m_b):
    tq = q_ref.shape[0]
    L = kc_ref.shape[0]
    span = tq + 2 * WINDOW
    i = pl.program_id(1)

    def band_start(tile):
        return pl.multiple_of(jnp.clip(tile * tq - WINDOW, 0, S - span), HEAD_DIM)

    def step(write, read):
        if write is not None:
            start = band_start(i)
            qpos = i * tq + lax.broadcasted_iota(jnp.int32, (tq, span), 0)
            kpos = start + lax.broadcasted_iota(jnp.int32, (tq, span), 1)
            bias = jnp.where(jnp.abs(kpos - qpos) <= WINDOW, 0.0, NEG_INF)
            bias = jnp.concatenate([bias] * GROUP, axis=0)
            for g in range(N_KV):
                cs, _ = _kv_slices(g)
                qs = _stack_heads(q_ref, g)
                s_ctx = _qk(qs, kc_ref[:, cs])
                s_lat = _qk(qs, kl_ref[pl.ds(start, span), cs]) + bias
                write[0][g, :, :L] = s_ctx
                write[0][g, :, L:] = s_lat
                mrun = _lane_fold(s_lat, _lane_fold(s_ctx, None, jnp.maximum), jnp.maximum)
                write[1][g] = jnp.maximum(_row_shift(mrun), _sink_lanes(sink_ref, g, tq))
        if read is not None:
            start = band_start(i - 1)
            for g in range(N_KV):
                _, vs = _kv_slices(g)
                m = read[1][g]
                o = jnp.dot(_shifted_exp2(read[0][g, :, :L], m), vc_ref[:, vs], preferred_element_type=F32)
                o = o + jnp.dot(_shifted_exp2(read[0][g, :, L:], m), vl_ref[pl.ds(start, span), vs],
                                preferred_element_type=F32)
                l = o[:, HEAD_DIM:] + jnp.exp2(_sink_lanes(sink_ref, g, tq) - m)
                _store_heads(o_ref, o[:, :HEAD_DIM] * (1.0 / l), g, tq)

    a, b = (s_a, m_a), (s_b, m_b)
    odd = i % 2 == 1
    steady = jnp.logical_and(i > 0, i < nlat)

    @pl.when(i == 0)
    def _():
        step(a, None)

    @pl.when(jnp.logical_and(steady, odd))
    def _():
        step(b, a)

    @pl.when(jnp.logical_and(steady, jnp.logical_not(odd)))
    def _():
        step(a, b)

    @pl.when(i == nlat)
    def _():
        step(None, b)


def _window_attention(q, k, v, sink, dims, tq, rows):
    B, S, L = dims
    nlat = S // tq
    assert nlat % 2 == 0
    M = GROUP * tq
    keys = L + tq + 2 * WINDOW
    lat_spec = lambda wd: pl.BlockSpec((S, wd), lambda b, i: (b, 0))
    ctx_spec = lambda wd: pl.BlockSpec((L, wd), lambda b, i: (B * S // L + b, 0))
    return pl.pallas_call(
        functools.partial(_window_kernel, nlat, S),
        grid=(B, nlat + 1),
        in_specs=[pl.BlockSpec(memory_space=pltpu.SMEM),
                  pl.BlockSpec((tq, D_MODEL), lambda b, i: (b * nlat + jnp.minimum(i, nlat - 1), 0)),
                  lat_spec(KV_W), lat_spec(2 * KV_W), ctx_spec(KV_W), ctx_spec(2 * KV_W)],
        out_specs=pl.BlockSpec((tq, D_MODEL), lambda b, i: (b * nlat + jnp.maximum(i - 1, 0), 0)),
        out_shape=jax.ShapeDtypeStruct((rows, D_MODEL), BF16),
        scratch_shapes=[pltpu.VMEM((N_KV, M, keys), F32)] * 2 + [pltpu.VMEM((N_KV, M, HEAD_DIM), F32)] * 2,
        compiler_params=_cparams(("arbitrary", "arbitrary")),
        name="attn_window",
    )(sink, q, k, v, k, v)


def _dense_kernel(nlat, q_ref, kl_ref, vl_ref, kc_ref, vc_ref, o_ref, s_a, s_b, m_a, m_b):
    tq = q_ref.shape[0]
    L = kc_ref.shape[0]
    S = kl_ref.shape[0]
    i = pl.program_id(1)
    bounds = [(0, L)] + [(L + j0, L + j0 + KEY_CHUNK) for j0 in range(0, S, KEY_CHUNK)]

    def step(write, read):
        for g in range(N_KV):
            if write is None:
                break
            cs, _ = _kv_slices(g)
            qs = _stack_heads(q_ref, g)
            mrun = None
            for lo, hi in bounds:
                k = kc_ref[:, cs] if lo == 0 else kl_ref[lo - L:hi - L, cs]
                s = _qk(qs, k)
                write[0][g, :, lo:hi] = s
                mrun = _lane_fold(s, mrun, jnp.maximum)
            write[1][g] = _row_shift(mrun)
        for g in range(N_KV):
            if read is None:
                break
            _, vs = _kv_slices(g)
            m = read[1][g]
            o = None
            for lo, hi in bounds:
                v = vc_ref[:, vs] if lo == 0 else vl_ref[lo - L:hi - L, vs]
                d = jnp.dot(_shifted_exp2(read[0][g, :, lo:hi], m), v, preferred_element_type=F32)
                o = d if o is None else o + d
            _store_heads(o_ref, o[:, :HEAD_DIM] * (1.0 / o[:, HEAD_DIM:]), g, tq)

    a, b = (s_a, m_a), (s_b, m_b)
    odd = i % 2 == 1
    steady = jnp.logical_and(i > 0, i < nlat)

    @pl.when(i == 0)
    def _():
        step(a, None)

    @pl.when(jnp.logical_and(steady, odd))
    def _():
        step(b, a)

    @pl.when(jnp.logical_and(steady, jnp.logical_not(odd)))
    def _():
        step(a, b)

    @pl.when(i == nlat)
    def _():
        step(None, b)


def _dense_attention(q, k, v, dims, tq, rows):
    B, S, L = dims
    nlat = S // tq
    assert nlat % 2 == 0
    one = pl.Buffered(1)
    return pl.pallas_call(
        functools.partial(_dense_kernel, nlat),
        grid=(B, nlat + 1),
        in_specs=[
            pl.BlockSpec((tq, D_MODEL), lambda b, i: (b * nlat + jnp.minimum(i, nlat - 1), 0)),
            pl.BlockSpec((S, KV_W), lambda b, i: (b, 0), pipeline_mode=one),
            pl.BlockSpec((S, 2 * KV_W), lambda b, i: (b, 0), pipeline_mode=one),
            pl.BlockSpec((L, KV_W), lambda b, i: (B * S // L + b, 0), pipeline_mode=one),
            pl.BlockSpec((L, 2 * KV_W), lambda b, i: (B * S // L + b, 0), pipeline_mode=one),
        ],
        out_specs=pl.BlockSpec((tq, D_MODEL), lambda b, i: (b * nlat + jnp.maximum(i - 1, 0), 0)),
        out_shape=jax.ShapeDtypeStruct((rows, D_MODEL), BF16),
        scratch_shapes=[pltpu.VMEM((N_KV, GROUP * tq, L + S), F32)] * 2
        + [pltpu.VMEM((N_KV, GROUP * tq, HEAD_DIM), F32)] * 2,
        compiler_params=_cparams(("arbitrary", "arbitrary")),
        name="attn_dense",
    )(q, k, v, k, v)


def _ctx_kernel(sink_ref, qb_ref, kb_ref, vb_ref, qc_ref, kc_ref, vc_ref, yb_any, yc_any, ob_ref, oc_ref):
    del yb_any, yc_any
    tq = qb_ref.shape[0]
    for g in range(N_KV):
        cs, vs = _kv_slices(g)
        qs = _stack_heads(qb_ref, g)
        ob = _softmax_pv([_qk(qs, kb_ref[:, cs])], [vb_ref[:, vs]], _sink_col(sink_ref, g, tq))
        _store_heads(ob_ref, ob, g, tq)
        qs = _stack_heads(qc_ref, g)
        oc = _softmax_pv([_qk(qs, kc_ref[:, cs])], [vc_ref[:, vs]], None)
        _store_heads(oc_ref, oc, g, tq)


def _ctx_attention(sink, qB, kB, vB, qC, kC, vC, yB, yC, dims):
    B, S, L = dims
    c0 = B * S // L
    rows = lambda wd: pl.BlockSpec((L, wd), lambda b: (c0 + b, 0))
    anyspec = pl.BlockSpec(memory_space=pl.ANY)
    return pl.pallas_call(
        _ctx_kernel,
        grid=(B,),
        in_specs=[pl.BlockSpec(memory_space=pltpu.SMEM),
                  rows(D_MODEL), rows(KV_W), rows(2 * KV_W),
                  rows(D_MODEL), rows(KV_W), rows(2 * KV_W),
                  anyspec, anyspec],
        out_specs=[rows(D_MODEL), rows(D_MODEL)],
        out_shape=[jax.ShapeDtypeStruct(yB.shape, BF16), jax.ShapeDtypeStruct(yC.shape, BF16)],
        input_output_aliases={7: 0, 8: 1},
        compiler_params=_cparams(("arbitrary",)),
        name="attn_ctx",
    )(sink, qB, kB, vB, qC, kC, vC, yB, yC)


def _merge_kernel(final, x_ref, mod_ref, yf_ref, yr_ref, yb_ref, yc_ref, ga_ref, gb_ref, gc_ref, m_ref,
                  wb_ref, wo_ref, fg_ref, o_ref):
    ys = (yf_ref[...] + yr_ref[...], yb_ref[...].astype(F32), yc_ref[...].astype(F32))
    gs = (ga_ref, gb_ref, gc_ref)
    acc = 0.0
    for n in range(3):
        t = (ys[n] * gs[n][...].astype(F32)).astype(BF16)
        br = jnp.dot(t, wb_ref[n], preferred_element_type=F32)
        acc = acc + m_ref[:, n * D_MODEL:(n + 1) * D_MODEL].astype(F32) * br
    y = jnp.dot(acc.astype(BF16), wo_ref[...], preferred_element_type=F32)
    xn = x_ref[...] + mod_ref[:, 2 * D_MODEL:] * y
    if final:
        xn = xn * lax.rsqrt(jnp.mean(xn * xn, axis=-1, keepdims=True) + EPS) * fg_ref[...]
    o_ref[...] = xn


def _merge(X, mod, yf, yr, yb, yc, ga, gb, gc, m, wb, wo, fg, dims, tm, final):
    B, S, L = dims
    n = B * S if final else X.shape[0]
    ts = S // tm
    nlat = B * ts

    def mod_idx(i):
        return (jnp.where(i < nlat, i // ts, B), 0, 0)

    rows = lambda wd: pl.BlockSpec((tm, wd), lambda i: (i, 0))
    return pl.pallas_call(
        functools.partial(_merge_kernel, final),
        grid=(n // tm,),
        in_specs=[
            rows(D_MODEL),
            pl.BlockSpec((None, 1, 3 * D_MODEL), mod_idx),
            rows(D_MODEL), rows(D_MODEL), rows(D_MODEL), rows(D_MODEL),
            rows(D_MODEL), rows(D_MODEL), rows(D_MODEL), rows(3 * D_MODEL),
            pl.BlockSpec((3, D_MODEL, D_MODEL), lambda i: (0, 0, 0), pipeline_mode=pl.Buffered(1)),
            pl.BlockSpec((D_MODEL, D_MODEL), lambda i: (0, 0), pipeline_mode=pl.Buffered(1)),
            pl.BlockSpec((1, D_MODEL), lambda i: (0, 0)),
        ],
        out_specs=rows(D_MODEL),
        out_shape=jax.ShapeDtypeStruct((n, D_MODEL), F32),
        compiler_params=_cparams(("parallel",)),
        name="merge_final" if final else "merge",
    )(X, mod, yf, yr, yb, yc, ga, gb, gc, m, wb, wo, fg)


def _rope_tables(S, tm):
    t = jnp.arange(S)
    P = HEAD_DIM // 4
    inv = ROPE_THETA ** (-jnp.arange(P, dtype=F32) / P)
    ang = jnp.stack([(t // GRID_W)[:, None] * inv, (t % GRID_W)[:, None] * inv], axis=1)
    cos = jnp.cos(ang).reshape(S, HEAD_DIM // 2)
    sin = jnp.sin(ang).reshape(S, HEAD_DIM // 2)
    c = jnp.concatenate([cos, cos], axis=1)
    s = jnp.concatenate([-sin, sin], axis=1)
    ident = jnp.zeros((tm, HEAD_DIM), F32)
    return jnp.concatenate([c, ident + 1.0]), jnp.concatenate([s, ident])


def kernel(x, c, ctx, c_ctx, norm_g, w_mod, b_mod, w_in, conv_w, conv_b, lru_wa, lru_ba, lru_wx, lru_bx,
           lru_lambda, attn_sink, q_norm_g, k_norm_g, w_branch, w_out, final_g):
    B, S, _ = x.shape
    L = ctx.shape[1]
    depth = w_in.shape[0]
    dims = (B, S, L)
    assert S % L == 0 and L % 128 == 0 and S % GRID_W == 0 and S >= 256 + 2 * WINDOW
    tm = math.gcd(1024, math.gcd(S, B * L))
    tm_merge = math.gcd(512, tm)
    tq_w = 256
    tq_g = 128

    X = jnp.concatenate([x.reshape(B * S, D_MODEL), ctx.reshape(B * L, D_MODEL)], axis=0)
    rows = -(-(B + 1) // 8) * 8
    cc = jnp.concatenate([c, c_ctx[None], jnp.zeros((rows - B - 1, D_MODEL), F32)], axis=0)
    mod_all = _modulation(cc, w_mod, b_mod).reshape(depth, rows, 1, 3 * D_MODEL)
    tabs = _rope_tables(S, tm)
    w_in16 = w_in.astype(BF16)
    cuts = (2 * D_MODEL, 3 * D_MODEL, 3 * D_MODEL + KV_W, 4 * D_MODEL + 2 * KV_W, 5 * D_MODEL + 2 * KV_W,
            5 * D_MODEL + 3 * KV_W)
    w_in16 = jnp.concatenate(
        [w_in16[..., :cuts[0]], _pair_layout(w_in16[..., cuts[0]:cuts[1]]), _pair_layout(w_in16[..., cuts[1]:cuts[2]]),
         w_in16[..., cuts[2]:cuts[3]], _pair_layout(w_in16[..., cuts[3]:cuts[4]]),
         _pair_layout(w_in16[..., cuts[4]:cuts[5]]), w_in16[..., cuts[5]:]], axis=-1)
    wb16 = w_branch.astype(BF16)
    wo16 = w_out.astype(BF16)
    row = lambda a: a.reshape(1, -1)

    a_specs = (("plain", D_MODEL), ("silu", D_MODEL))
    b_specs = (("rope_q", D_MODEL), ("rope_k", KV_W), ("value", KV_W), ("silu", D_MODEL))
    c_specs = (("normrope_q", D_MODEL), ("normrope_k", KV_W), ("value", KV_W), ("silu", D_MODEL))
    m_specs = (("sigmoid", 3 * D_MODEL),)
    a_w = 2 * D_MODEL
    b_w = 2 * D_MODEL + 2 * KV_W

    out = None
    for l in range(depth):
        last = l == depth - 1
        mod = mod_all[l]
        g = row(norm_g[l])
        qg, kg = _pair_layout(row(q_norm_g[l])), _pair_layout(row(k_norm_g[l]))
        w = w_in16[l]
        proj = functools.partial(_inproj, X, mod, g, tabs=tabs, qg=qg, kg=kg, dims=dims, tm=tm)
        uA, gA = proj(w=w[:, :a_w], specs=a_specs, dtypes=(F32, BF16), name="inproj_a")
        qB, kB, vB, gB = proj(w=w[:, a_w:a_w + b_w], specs=b_specs, dtypes=(BF16,) * 4, name="inproj_b")
        qC, kC, vC, gC = proj(w=w[:, a_w + b_w:a_w + 2 * b_w], specs=c_specs, dtypes=(BF16,) * 4,
                              name="inproj_c")
        (m,) = proj(w=w[:, a_w + 2 * b_w:], specs=m_specs, dtypes=(BF16,), name="inproj_m")

        ys = []
        for d, rev in enumerate((False, True)):
            wg = _gate_weights(lru_wa[l, d], lru_wx[l, d])
            weights = (conv_w[l], row(conv_b[l]), wg, row(lru_ba[l, d]), row(lru_bx[l, d]), row(lru_lambda[l, d]))
            ys.append(_lru(uA, weights, dims, rev, "lru_rev" if rev else "lru_fwd"))
        rows_out = B * S if last else X.shape[0]
        yB = _window_attention(qB, kB, vB, attn_sink[l], dims, tq_w, rows_out)
        yC = _dense_attention(qC, kC, vC, dims, tq_g, rows_out)
        if not last:
            yB, yC = _ctx_attention(attn_sink[l], qB, kB, vB, qC, kC, vC, yB, yC, dims)
        res = _merge(X, mod, ys[0], ys[1], yB, yC, gA, gB, gC, m, wb16[l], wo16[l], row(final_g),
                     dims, tm_merge, last)
        if last:
            out = res.reshape(B, S, D_MODEL)
        else:
            X = res
    return out
```

```python
import functools
import math

import jax
import jax.numpy as jnp
from jax import lax
from jax.experimental import pallas as pl
from jax.experimental.pallas import tpu as pltpu

D_MODEL = 1024
HEAD_DIM = 128
N_HEADS = D_MODEL // HEAD_DIM
N_KV = N_HEADS // 4
GROUP = N_HEADS // N_KV
KV_W = N_KV * HEAD_DIM
LRU_BLOCK_W = 64
LRU_C = 8.0
CONV_W = 4
CONV_LEFT = 2
WINDOW = 128
GRID_W = 64
ROPE_THETA = 10000.0
EPS = 1e-6
NEG_INF = -1e30
LOG2E = 1.4426950408889634
Q_SCALE = HEAD_DIM ** -0.5 * LOG2E
GATE_CHUNK = 256
KEY_CHUNK = 512
LRU_BATCH = 4
VMEM_LIMIT = 56 * 1024 * 1024

F32 = jnp.float32
BF16 = jnp.bfloat16


def _cparams(sem):
    return pltpu.CompilerParams(dimension_semantics=sem, vmem_limit_bytes=VMEM_LIMIT)


def _sigmoid(x):
    return 0.5 * (jnp.tanh(0.5 * x) + 1.0)


def _mod_kernel(cc_ref, w_ref, b_ref, o_ref):
    cc = cc_ref[...]
    s = cc * _sigmoid(cc)
    o_ref[...] = jnp.dot(s, w_ref[...], preferred_element_type=F32,
                         precision=lax.Precision.HIGHEST) + b_ref[...]


def _modulation(cc, w_mod, b_mod):
    depth = w_mod.shape[0]
    rows = cc.shape[0]
    return pl.pallas_call(
        _mod_kernel,
        grid=(depth, 3),
        in_specs=[
            pl.BlockSpec((rows, D_MODEL), lambda l, j: (0, 0)),
            pl.BlockSpec((None, D_MODEL, D_MODEL), lambda l, j: (l, 0, j)),
            pl.BlockSpec((None, 1, D_MODEL), lambda l, j: (l, 0, j)),
        ],
        out_specs=pl.BlockSpec((None, rows, D_MODEL), lambda l, j: (l, 0, j)),
        out_shape=jax.ShapeDtypeStruct((depth, rows, 3 * D_MODEL), F32),
        compiler_params=_cparams(("arbitrary", "arbitrary")),
        name="modulation",
    )(cc, w_mod, b_mod.reshape(depth, 1, 3 * D_MODEL))


def _norm_mod(x, g, mod):
    ms = jnp.mean(x * x, axis=-1, keepdims=True)
    y = x * lax.rsqrt(ms + EPS) * g
    return y * (1.0 + mod[:, D_MODEL:2 * D_MODEL]) + mod[:, :D_MODEL]


def _pair_layout(w):
    lead = w.shape[:-1]
    nh = w.shape[-1] // HEAD_DIM
    return w.reshape(*lead, nh, 2, 2, HEAD_DIM // 4).swapaxes(-3, -2).reshape(*lead, nh * HEAD_DIM)


def _rope(yh, c, s):
    return yh * c + pltpu.roll(yh, HEAD_DIM // 2, 1) * s


def _head_norm(yh, g):
    avg = jnp.full((HEAD_DIM, HEAD_DIM), 1.0 / HEAD_DIM, BF16)
    ms = jnp.dot((yh * yh).astype(BF16), avg, preferred_element_type=F32)
    return yh * lax.rsqrt(ms + EPS) * g


def _inproj_kernel(specs, x_ref, mod_ref, g_ref, w_ref, c_ref, s_ref, qg_ref, kg_ref, *out_refs):
    h = _norm_mod(x_ref[...], g_ref[...], mod_ref[...]).astype(BF16)
    off = 0
    for (kind, width), o_ref in zip(specs, out_refs):
        y = jnp.dot(h, w_ref[:, off:off + width], preferred_element_type=F32)
        off += width
        if kind == "plain":
            o_ref[...] = y.astype(o_ref.dtype)
        elif kind == "value":
            for hh in range(width // HEAD_DIM):
                o_ref[:, 2 * hh * HEAD_DIM:(2 * hh + 1) * HEAD_DIM] = (
                    y[:, hh * HEAD_DIM:(hh + 1) * HEAD_DIM].astype(o_ref.dtype))
                o_ref[:, (2 * hh + 1) * HEAD_DIM:(2 * hh + 2) * HEAD_DIM] = jnp.ones(
                    (y.shape[0], HEAD_DIM), o_ref.dtype)
        elif kind == "silu":
            o_ref[...] = (y * _sigmoid(y)).astype(o_ref.dtype)
        elif kind == "sigmoid":
            o_ref[...] = _sigmoid(y).astype(o_ref.dtype)
        else:
            is_q = kind in ("rope_q", "normrope_q")
            norm = kind in ("normrope_q", "normrope_k")
            c, s = c_ref[...], s_ref[...]
            for hh in range(width // HEAD_DIM):
                sl = slice(hh * HEAD_DIM, (hh + 1) * HEAD_DIM)
                yh = y[:, sl]
                if norm:
                    yh = _head_norm(yh, qg_ref[...] if is_q else kg_ref[...])
                yh = _rope(yh, c, s)
                if is_q:
                    yh = yh * Q_SCALE
                o_ref[:, sl] = yh.astype(o_ref.dtype)


def _inproj(X, mod, g, w, tabs, qg, kg, specs, dtypes, dims, tm, name):
    B, S, L = dims
    n = X.shape[0]
    ts = S // tm
    nlat = B * ts
    wtot = w.shape[1]

    def mod_idx(i):
        return (jnp.where(i < nlat, i // ts, B), 0, 0)

    def tab_idx(i):
        return (jnp.where(i < nlat, i % ts, ts), 0)

    widths = [2 * wd if kind == "value" else wd for kind, wd in specs]
    out_shape = [jax.ShapeDtypeStruct((n, wd), dt) for wd, dt in zip(widths, dtypes)]
    out_specs = [pl.BlockSpec((tm, wd), lambda i: (i, 0)) for wd in widths]
    tab_spec = pl.BlockSpec((tm, HEAD_DIM), tab_idx)
    return pl.pallas_call(
        functools.partial(_inproj_kernel, specs),
        grid=(n // tm,),
        in_specs=[
            pl.BlockSpec((tm, D_MODEL), lambda i: (i, 0)),
            pl.BlockSpec((None, 1, 3 * D_MODEL), mod_idx),
            pl.BlockSpec((1, D_MODEL), lambda i: (0, 0)),
            pl.BlockSpec((D_MODEL, wtot), lambda i: (0, 0)),
            tab_spec, tab_spec,
            pl.BlockSpec((1, HEAD_DIM), lambda i: (0, 0)),
            pl.BlockSpec((1, HEAD_DIM), lambda i: (0, 0)),
        ],
        out_specs=out_specs,
        out_shape=out_shape,
        compiler_params=_cparams(("parallel",)),
        name=name,
    )(X, mod, g, w, tabs[0], tabs[1], qg, kg)


def _lru_kernel(reverse, nchunk, u_ref, up_ref, un_ref, h0_ref, cw_ref, cb_ref, wg_ref, ba_ref, bx_ref, lam_ref,
                *rest):
    y_ref, hout_ref, ext_s, a_s, b_s, h_s = rest[-6:]
    K, T, _ = u_ref.shape
    c = pl.program_id(1)
    j = (nchunk - 1 - c) if reverse else c
    has_prev = j > 0
    has_next = j < nchunk - 1

    lam = lam_ref[...]
    neg = -lam
    softplus = jnp.maximum(neg, 0.0) + jnp.log1p(jnp.exp(-jnp.abs(neg)))
    c1 = (-0.5 * LRU_C * LOG2E) * softplus
    hba = 0.5 * ba_ref[...]
    hbx = 0.5 * bx_ref[...]
    cw = cw_ref[...]

    for kb in range(K):
        ext_s[kb, 0:8, :] = jnp.where(has_prev, up_ref[kb], 0.0)
        ext_s[kb, 8:8 + T, :] = u_ref[kb]
        ext_s[kb, 8 + T:16 + T, :] = jnp.where(has_next, un_ref[kb], 0.0)
        ext = ext_s[kb]
        u = cb_ref[...]
        for k in range(CONV_W):
            shift = (CONV_LEFT - k) % (T + 16)
            tap = ext if shift == 0 else pltpu.roll(ext, shift, 0)
            u = u + tap[8:8 + T, :] * cw[k:k + 1, :]
        ub = u.astype(BF16)
        hu = 0.5 * u
        for q in range(D_MODEL // GATE_CHUNK):
            sl = slice(q * GATE_CHUNK, (q + 1) * GATE_CHUNK)
            z = jnp.dot(ub[:, sl], wg_ref[q], preferred_element_type=F32)
            tr = jnp.tanh(z[:, :GATE_CHUNK] + hba[:, sl])
            ti = jnp.tanh(z[:, GATE_CHUNK:] + hbx[:, sl])
            a = jnp.exp2(c1[:, sl] * tr + c1[:, sl])
            a_s[kb, :, sl] = a
            b_s[kb, :, sl] = jnp.sqrt(1.0 - a * a) * ((ti + 1.0) * hu[:, sl])

    @pl.when(c == 0)
    def _():
        h_s[...] = h0_ref[...]

    def body(t, hs):
        tt = (T - 1 - t) if reverse else t
        out = []
        for kb in range(K):
            h = a_s[kb, pl.ds(tt, 1), :] * hs[kb] + b_s[kb, pl.ds(tt, 1), :]
            b_s[kb, pl.ds(tt, 1), :] = h
            out.append(h)
        return tuple(out)

    hs = lax.fori_loop(0, T, body, tuple(h_s[kb:kb + 1, :] for kb in range(K)), unroll=4)
    for kb in range(K):
        h_s[kb:kb + 1, :] = hs[kb]
        y_ref[kb] = b_s[kb].astype(y_ref.dtype)

    @pl.when(c == nchunk - 1)
    def _():
        hout_ref[...] = h_s[...]


def _lru_call(u3, h0, y_prev, weights, K, T, seg0, reverse, name):
    rows = u3.shape[1]
    nchunk = rows // T
    t8 = T // 8
    blk0 = seg0 // K

    def chunk(c):
        return (nchunk - 1 - c) if reverse else c

    main = pl.BlockSpec((K, T, D_MODEL), lambda g, c: (blk0 + g, chunk(c), 0))
    prev = pl.BlockSpec((K, 8, D_MODEL), lambda g, c: (blk0 + g, jnp.maximum(chunk(c) * t8 - 1, 0), 0))
    nxt = pl.BlockSpec((K, 8, D_MODEL),
                       lambda g, c: (blk0 + g, jnp.minimum((chunk(c) + 1) * t8, rows // 8 - 1), 0))
    hspec = pl.BlockSpec((None, K, D_MODEL), lambda g, c: (g, 0, 0))
    fixed = lambda shape: pl.BlockSpec(shape, lambda g, c: (0,) * len(shape))
    in_specs = [main, prev, nxt, hspec, fixed((CONV_W, D_MODEL)), fixed((1, D_MODEL)),
                fixed((D_MODEL // GATE_CHUNK, GATE_CHUNK, 2 * GATE_CHUNK)),
                fixed((1, D_MODEL)), fixed((1, D_MODEL)), fixed((1, D_MODEL))]
    args = [u3, u3, u3, h0, *weights]
    aliases = {}
    if y_prev is not None:
        in_specs.append(pl.BlockSpec(memory_space=pl.ANY))
        args.append(y_prev)
        aliases = {len(args) - 1: 0}
    return pl.pallas_call(
        functools.partial(_lru_kernel, reverse, nchunk),
        grid=(h0.shape[0], nchunk),
        in_specs=in_specs,
        out_specs=[main, hspec],
        out_shape=[jax.ShapeDtypeStruct(u3.shape, BF16), jax.ShapeDtypeStruct(h0.shape, F32)],
        input_output_aliases=aliases,
        scratch_shapes=[
            pltpu.VMEM((K, T + 16, D_MODEL), F32),
            pltpu.VMEM((K, T, D_MODEL), F32),
            pltpu.VMEM((K, T, D_MODEL), F32),
            pltpu.VMEM((K, D_MODEL), F32),
        ],
        compiler_params=_cparams(("arbitrary", "arbitrary")),
        name=name,
    )(*args)


def _lru(uA, weights, dims, reverse, name):
    B, S, L = dims
    n = uA.shape[0]
    K = LRU_BATCH
    assert B % K == 0 and n % S == 0 and (B * S // L) % K == 0
    h0 = jnp.zeros((B // K, K, D_MODEL), F32)
    y, h = _lru_call(uA.reshape(n // L, L, D_MODEL), h0, None, weights, K, L, B * S // L, reverse, name + "_ctx")
    y, _ = _lru_call(uA.reshape(n // S, S, D_MODEL), h, y.reshape(n // S, S, D_MODEL), weights, K, L, 0, reverse,
                     name)
    return y.reshape(n, D_MODEL)


def _gate_weights(wa, wx):
    per = GATE_CHUNK // LRU_BLOCK_W
    nchunks = wa.shape[0] // per
    eye = jnp.eye(per, dtype=wa.dtype)

    def bd(w):
        w4 = w.reshape(nchunks, per, LRU_BLOCK_W, LRU_BLOCK_W)
        return jnp.einsum("jkde,kl->jkdle", w4, eye).reshape(nchunks, GATE_CHUNK, GATE_CHUNK)

    return (0.5 * jnp.concatenate([bd(wa), bd(wx)], axis=-1)).astype(BF16)


def _row_shift(mrun):
    mb = mrun.astype(BF16)
    return jnp.broadcast_to(mb.max(axis=-1, keepdims=True), mb.shape).astype(F32)


def _shifted_exp2(s, shift):
    blocks = [jnp.exp2(s[:, c * HEAD_DIM:(c + 1) * HEAD_DIM] - shift) for c in range(s.shape[1] // HEAD_DIM)]
    return jnp.concatenate(blocks, axis=1).astype(BF16)


def _lane_fold(x, acc, op):
    for c in range(x.shape[1] // HEAD_DIM):
        blk = x[:, c * HEAD_DIM:(c + 1) * HEAD_DIM]
        acc = blk if acc is None else op(acc, blk)
    return acc


def _softmax_pv(scores, values, sink):
    mrun = None
    for s in scores:
        mrun = _lane_fold(s, mrun, jnp.maximum)
    m = mrun.max(axis=-1, keepdims=True)
    if sink is not None:
        m = jnp.maximum(m, sink)
    o = None
    for s, v in zip(scores, values):
        d = jnp.dot(jnp.exp2(s - m).astype(BF16), v, preferred_element_type=F32)
        o = d if o is None else o + d
    l = o[:, HEAD_DIM:]
    if sink is not None:
        l = l + jnp.exp2(sink - m)
    return o[:, :HEAD_DIM] * (1.0 / l)


def _qk(q, k):
    return lax.dot_general(q, k, (((1,), (1,)), ((), ())), preferred_element_type=F32)


def _stack_heads(q_ref, g):
    return jnp.concatenate(
        [q_ref[:, (g * GROUP + h) * HEAD_DIM:(g * GROUP + h + 1) * HEAD_DIM] for h in range(GROUP)], axis=0)


def _sink_col(sink_ref, g, tq):
    return jnp.concatenate(
        [jnp.full((tq, 1), sink_ref[g * GROUP + h] * LOG2E, F32) for h in range(GROUP)], axis=0)


def _sink_lanes(sink_ref, g, tq):
    return jnp.concatenate(
        [jnp.full((tq, HEAD_DIM), sink_ref[g * GROUP + h] * LOG2E, F32) for h in range(GROUP)], axis=0)


def _store_heads(o_ref, gate_ref, o, g, tq):
    for h in range(GROUP):
        sl = slice((g * GROUP + h) * HEAD_DIM, (g * GROUP + h + 1) * HEAD_DIM)
        o_ref[:, sl] = (o[h * tq:(h + 1) * tq] * gate_ref[:, sl].astype(F32)).astype(o_ref.dtype)


def _kv_slices(g):
    return slice(g * HEAD_DIM, (g + 1) * HEAD_DIM), slice(2 * g * HEAD_DIM, 2 * (g + 1) * HEAD_DIM)


def _window_kernel(nlat, S, sink_ref, q_ref, kl_ref, vl_ref, kc_ref, vc_ref, gate_ref, o_ref, s_a, s_b, m_a, m_b):
    tq = q_ref.shape[0]
    L = kc_ref.shape[0]
    span = tq + 2 * WINDOW
    i = pl.program_id(1)

    def band_start(tile):
        return pl.multiple_of(jnp.clip(tile * tq - WINDOW, 0, S - span), HEAD_DIM)

    def step(write, read):
        if write is not None:
            start = band_start(i)
            qpos = i * tq + lax.broadcasted_iota(jnp.int32, (tq, span), 0)
            kpos = start + lax.broadcasted_iota(jnp.int32, (tq, span), 1)
            bias = jnp.where(jnp.abs(kpos - qpos) <= WINDOW, 0.0, NEG_INF)
            bias = jnp.concatenate([bias] * GROUP, axis=0)
            for g in range(N_KV):
                cs, _ = _kv_slices(g)
                qs = _stack_heads(q_ref, g)
                s_ctx = _qk(qs, kc_ref[:, cs])
                s_lat = _qk(qs, kl_ref[pl.ds(start, span), cs]) + bias
                write[0][g, :, :L] = s_ctx
                write[0][g, :, L:] = s_lat
                mrun = _lane_fold(s_lat, _lane_fold(s_ctx, None, jnp.maximum), jnp.maximum)
                write[1][g] = jnp.maximum(_row_shift(mrun), _sink_lanes(sink_ref, g, tq))
        if read is not None:
            start = band_start(i - 1)
            for g in range(N_KV):
                _, vs = _kv_slices(g)
                m = read[1][g]
                o = jnp.dot(_shifted_exp2(read[0][g, :, :L], m), vc_ref[:, vs], preferred_element_type=F32)
                o = o + jnp.dot(_shifted_exp2(read[0][g, :, L:], m), vl_ref[pl.ds(start, span), vs],
                                preferred_element_type=F32)
                l = o[:, HEAD_DIM:] + jnp.exp2(_sink_lanes(sink_ref, g, tq) - m)
                _store_heads(o_ref, gate_ref, o[:, :HEAD_DIM] * (1.0 / l), g, tq)

    a, b = (s_a, m_a), (s_b, m_b)
    odd = i % 2 == 1
    steady = jnp.logical_and(i > 0, i < nlat)

    @pl.when(i == 0)
    def _():
        step(a, None)

    @pl.when(jnp.logical_and(steady, odd))
    def _():
        step(b, a)

    @pl.when(jnp.logical_and(steady, jnp.logical_not(odd)))
    def _():
        step(a, b)

    @pl.when(i == nlat)
    def _():
        step(None, b)


def _window_attention(q, k, v, gate, sink, dims, tq, rows):
    B, S, L = dims
    nlat = S // tq
    assert nlat % 2 == 0
    M = GROUP * tq
    keys = L + tq + 2 * WINDOW
    lat_spec = lambda wd: pl.BlockSpec((S, wd), lambda b, i: (b, 0))
    ctx_spec = lambda wd: pl.BlockSpec((L, wd), lambda b, i: (B * S // L + b, 0))
    return pl.pallas_call(
        functools.partial(_window_kernel, nlat, S),
        grid=(B, nlat + 1),
        in_specs=[pl.BlockSpec(memory_space=pltpu.SMEM),
                  pl.BlockSpec((tq, D_MODEL), lambda b, i: (b * nlat + jnp.minimum(i, nlat - 1), 0)),
                  lat_spec(KV_W), lat_spec(2 * KV_W), ctx_spec(KV_W), ctx_spec(2 * KV_W),
                  pl.BlockSpec((tq, D_MODEL), lambda b, i: (b * nlat + jnp.maximum(i - 1, 0), 0))],
        out_specs=pl.BlockSpec((tq, D_MODEL), lambda b, i: (b * nlat + jnp.maximum(i - 1, 0), 0)),
        out_shape=jax.ShapeDtypeStruct((rows, D_MODEL), BF16),
        scratch_shapes=[pltpu.VMEM((N_KV, M, keys), F32)] * 2 + [pltpu.VMEM((N_KV, M, HEAD_DIM), F32)] * 2,
        compiler_params=_cparams(("arbitrary", "arbitrary")),
        name="attn_window",
    )(sink, q, k, v, k, v, gate)


def _dense_kernel(nlat, q_ref, kl_ref, vl_ref, kc_ref, vc_ref, gate_ref, o_ref, s_a, s_b, m_a, m_b):
    tq = q_ref.shape[0]
    L = kc_ref.shape[0]
    S = kl_ref.shape[0]
    i = pl.program_id(1)
    bounds = [(0, L)] + [(L + j0, L + j0 + KEY_CHUNK) for j0 in range(0, S, KEY_CHUNK)]

    def step(write, read):
        for g in range(N_KV):
            if write is None:
                break
            cs, _ = _kv_slices(g)
            qs = _stack_heads(q_ref, g)
            mrun = None
            for lo, hi in bounds:
                k = kc_ref[:, cs] if lo == 0 else kl_ref[lo - L:hi - L, cs]
                s = _qk(qs, k)
                write[0][g, :, lo:hi] = s
                mrun = _lane_fold(s, mrun, jnp.maximum)
            write[1][g] = _row_shift(mrun)
        for g in range(N_KV):
            if read is None:
                break
            _, vs = _kv_slices(g)
            m = read[1][g]
            o = None
            for lo, hi in bounds:
                v = vc_ref[:, vs] if lo == 0 else vl_ref[lo - L:hi - L, vs]
                d = jnp.dot(_shifted_exp2(read[0][g, :, lo:hi], m), v, preferred_element_type=F32)
                o = d if o is None else o + d
            _store_heads(o_ref, gate_ref, o[:, :HEAD_DIM] * (1.0 / o[:, HEAD_DIM:]), g, tq)

    a, b = (s_a, m_a), (s_b, m_b)
    odd = i % 2 == 1
    steady = jnp.logical_and(i > 0, i < nlat)

    @pl.when(i == 0)
    def _():
        step(a, None)

    @pl.when(jnp.logical_and(steady, odd))
    def _():
        step(b, a)

    @pl.when(jnp.logical_and(steady, jnp.logical_not(odd)))
    def _():
        step(a, b)

    @pl.when(i == nlat)
    def _():
        step(None, b)


def _dense_attention(q, k, v, gate, dims, tq, rows):
    B, S, L = dims
    nlat = S // tq
    assert nlat % 2 == 0
    one = pl.Buffered(1)
    return pl.pallas_call(
        functools.partial(_dense_kernel, nlat),
        grid=(B, nlat + 1),
        in_specs=[
            pl.BlockSpec((tq, D_MODEL), lambda b, i: (b * nlat + jnp.minimum(i, nlat - 1), 0)),
            pl.BlockSpec((S, KV_W), lambda b, i: (b, 0), pipeline_mode=one),
            pl.BlockSpec((S, 2 * KV_W), lambda b, i: (b, 0), pipeline_mode=one),
            pl.BlockSpec((L, KV_W), lambda b, i: (B * S // L + b, 0), pipeline_mode=one),
            pl.BlockSpec((L, 2 * KV_W), lambda b, i: (B * S // L + b, 0), pipeline_mode=one),
            pl.BlockSpec((tq, D_MODEL), lambda b, i: (b * nlat + jnp.maximum(i - 1, 0), 0)),
        ],
        out_specs=pl.BlockSpec((tq, D_MODEL), lambda b, i: (b * nlat + jnp.maximum(i - 1, 0), 0)),
        out_shape=jax.ShapeDtypeStruct((rows, D_MODEL), BF16),
        scratch_shapes=[pltpu.VMEM((N_KV, GROUP * tq, L + S), F32)] * 2
        + [pltpu.VMEM((N_KV, GROUP * tq, HEAD_DIM), F32)] * 2,
        compiler_params=_cparams(("arbitrary", "arbitrary")),
        name="attn_dense",
    )(q, k, v, k, v, gate)


def _ctx_kernel(sink_ref, qb_ref, kb_ref, vb_ref, gb_ref, qc_ref, kc_ref, vc_ref, gc_ref, yb_any, yc_any,
                ob_ref, oc_ref):
    del yb_any, yc_any
    tq = qb_ref.shape[0]
    for g in range(N_KV):
        cs, vs = _kv_slices(g)
        qs = _stack_heads(qb_ref, g)
        ob = _softmax_pv([_qk(qs, kb_ref[:, cs])], [vb_ref[:, vs]], _sink_col(sink_ref, g, tq))
        _store_heads(ob_ref, gb_ref, ob, g, tq)
        qs = _stack_heads(qc_ref, g)
        oc = _softmax_pv([_qk(qs, kc_ref[:, cs])], [vc_ref[:, vs]], None)
        _store_heads(oc_ref, gc_ref, oc, g, tq)


def _ctx_attention(sink, qB, kB, vB, gB, qC, kC, vC, gC, yB, yC, dims):
    B, S, L = dims
    c0 = B * S // L
    rows = lambda wd: pl.BlockSpec((L, wd), lambda b: (c0 + b, 0))
    anyspec = pl.BlockSpec(memory_space=pl.ANY)
    return pl.pallas_call(
        _ctx_kernel,
        grid=(B,),
        in_specs=[pl.BlockSpec(memory_space=pltpu.SMEM),
                  rows(D_MODEL), rows(KV_W), rows(2 * KV_W), rows(D_MODEL),
                  rows(D_MODEL), rows(KV_W), rows(2 * KV_W), rows(D_MODEL),
                  anyspec, anyspec],
        out_specs=[rows(D_MODEL), rows(D_MODEL)],
        out_shape=[jax.ShapeDtypeStruct(yB.shape, BF16), jax.ShapeDtypeStruct(yC.shape, BF16)],
        input_output_aliases={9: 0, 10: 1},
        compiler_params=_cparams(("arbitrary",)),
        name="attn_ctx",
    )(sink, qB, kB, vB, gB, qC, kC, vC, gC, yB, yC)


def _merge_kernel(final, x_ref, mod_ref, yf_ref, yr_ref, ga_ref, yb_ref, yc_ref, m_ref,
                  wb_ref, wo_ref, fg_ref, o_ref):
    ya = (yf_ref[...].astype(F32) + yr_ref[...].astype(F32)) * ga_ref[...].astype(F32)
    ts = (ya.astype(BF16), yb_ref[...], yc_ref[...])
    acc = 0.0
    for n in range(3):
        br = jnp.dot(ts[n], wb_ref[n], preferred_element_type=F32)
        acc = acc + m_ref[:, n * D_MODEL:(n + 1) * D_MODEL].astype(F32) * br
    y = jnp.dot(acc.astype(BF16), wo_ref[...], preferred_element_type=F32)
    xn = x_ref[...] + mod_ref[:, 2 * D_MODEL:] * y
    if final:
        xn = xn * lax.rsqrt(jnp.mean(xn * xn, axis=-1, keepdims=True) + EPS) * fg_ref[...]
    o_ref[...] = xn


def _merge(X, mod, yf, yr, ga, yb, yc, m, wb, wo, fg, dims, tm, final):
    B, S, L = dims
    n = B * S if final else X.shape[0]
    ts = S // tm
    nlat = B * ts

    def mod_idx(i):
        return (jnp.where(i < nlat, i // ts, B), 0, 0)

    rows = lambda wd: pl.BlockSpec((tm, wd), lambda i: (i, 0))
    return pl.pallas_call(
        functools.partial(_merge_kernel, final),
        grid=(n // tm,),
        in_specs=[
            rows(D_MODEL),
            pl.BlockSpec((None, 1, 3 * D_MODEL), mod_idx),
            rows(D_MODEL), rows(D_MODEL), rows(D_MODEL), rows(D_MODEL), rows(D_MODEL), rows(3 * D_MODEL),
            pl.BlockSpec((3, D_MODEL, D_MODEL), lambda i: (0, 0, 0), pipeline_mode=pl.Buffered(1)),
            pl.BlockSpec((D_MODEL, D_MODEL), lambda i: (0, 0), pipeline_mode=pl.Buffered(1)),
            pl.BlockSpec((1, D_MODEL), lambda i: (0, 0)),
        ],
        out_specs=rows(D_MODEL),
        out_shape=jax.ShapeDtypeStruct((n, D_MODEL), F32),
        compiler_params=_cparams(("parallel",)),
        name="merge_final" if final else "merge",
    )(X, mod, yf, yr, ga, yb, yc, m, wb, wo, fg)


def _rope_tables(S, tm):
    t = jnp.arange(S)
    P = HEAD_DIM // 4
    inv = ROPE_THETA ** (-jnp.arange(P, dtype=F32) / P)
    ang = jnp.stack([(t // GRID_W)[:, None] * inv, (t % GRID_W)[:, None] * inv], axis=1)
    cos = jnp.cos(ang).reshape(S, HEAD_DIM // 2)
    sin = jnp.sin(ang).reshape(S, HEAD_DIM // 2)
    c = jnp.concatenate([cos, cos], axis=1)
    s = jnp.concatenate([-sin, sin], axis=1)
    ident = jnp.zeros((tm, HEAD_DIM), F32)
    return jnp.concatenate([c, ident + 1.0]), jnp.concatenate([s, ident])


def kernel(x, c, ctx, c_ctx, norm_g, w_mod, b_mod, w_in, conv_w, conv_b, lru_wa, lru_ba, lru_wx, lru_bx,
           lru_lambda, attn_sink, q_norm_g, k_norm_g, w_branch, w_out, final_g):
    B, S, _ = x.shape
    L = ctx.shape[1]
    depth = w_in.shape[0]
    dims = (B, S, L)
    assert S % L == 0 and L % 128 == 0 and S % GRID_W == 0 and S >= 256 + 2 * WINDOW
    tm = math.gcd(1024, math.gcd(S, B * L))
    tm_merge = math.gcd(512, tm)
    tq_w = 256
    tq_g = 128

    X = jnp.concatenate([x.reshape(B * S, D_MODEL), ctx.reshape(B * L, D_MODEL)], axis=0)
    rows = -(-(B + 1) // 8) * 8
    cc = jnp.concatenate([c, c_ctx[None], jnp.zeros((rows - B - 1, D_MODEL), F32)], axis=0)
    mod_all = _modulation(cc, w_mod, b_mod).reshape(depth, rows, 1, 3 * D_MODEL)
    tabs = _rope_tables(S, tm)
    w_in16 = w_in.astype(BF16)
    cuts = (2 * D_MODEL, 3 * D_MODEL, 3 * D_MODEL + KV_W, 4 * D_MODEL + 2 * KV_W, 5 * D_MODEL + 2 * KV_W,
            5 * D_MODEL + 3 * KV_W)
    w_in16 = jnp.concatenate(
        [w_in16[..., :cuts[0]], _pair_layout(w_in16[..., cuts[0]:cuts[1]]), _pair_layout(w_in16[..., cuts[1]:cuts[2]]),
         w_in16[..., cuts[2]:cuts[3]], _pair_layout(w_in16[..., cuts[3]:cuts[4]]),
         _pair_layout(w_in16[..., cuts[4]:cuts[5]]), w_in16[..., cuts[5]:]], axis=-1)
    wb16 = w_branch.astype(BF16)
    wo16 = w_out.astype(BF16)
    row = lambda a: a.reshape(1, -1)

    a_specs = (("plain", D_MODEL), ("silu", D_MODEL))
    b_specs = (("rope_q", D_MODEL), ("rope_k", KV_W), ("value", KV_W), ("silu", D_MODEL))
    c_specs = (("normrope_q", D_MODEL), ("normrope_k", KV_W), ("value", KV_W), ("silu", D_MODEL))
    m_specs = (("sigmoid", 3 * D_MODEL),)
    a_w = 2 * D_MODEL
    b_w = 2 * D_MODEL + 2 * KV_W

    out = None
    for l in range(depth):
        last = l == depth - 1
        mod = mod_all[l]
        g = row(norm_g[l])
        qg, kg = _pair_layout(row(q_norm_g[l])), _pair_layout(row(k_norm_g[l]))
        w = w_in16[l]
        proj = functools.partial(_inproj, X, mod, g, tabs=tabs, qg=qg, kg=kg, dims=dims, tm=tm)
        uA, gA = proj(w=w[:, :a_w], specs=a_specs, dtypes=(F32, BF16), name="inproj_a")
        qB, kB, vB, gB = proj(w=w[:, a_w:a_w + b_w], specs=b_specs, dtypes=(BF16,) * 4, name="inproj_b")
        qC, kC, vC, gC = proj(w=w[:, a_w + b_w:a_w + 2 * b_w], specs=c_specs, dtypes=(BF16,) * 4,
                              name="inproj_c")
        (m,) = proj(w=w[:, a_w + 2 * b_w:], specs=m_specs, dtypes=(BF16,), name="inproj_m")

        ys = []
        for d, rev in enumerate((False, True)):
            wg = _gate_weights(lru_wa[l, d], lru_wx[l, d])
            weights = (conv_w[l], row(conv_b[l]), wg, row(lru_ba[l, d]), row(lru_bx[l, d]), row(lru_lambda[l, d]))
            ys.append(_lru(uA, weights, dims, rev, "lru_rev" if rev else "lru_fwd"))
        rows_out = B * S if last else X.shape[0]
        yB = _window_attention(qB, kB, vB, gB, attn_sink[l], dims, tq_w, rows_out)
        yC = _dense_attention(qC, kC, vC, gC, dims, tq_g, rows_out)
        if not last:
            yB, yC = _ctx_attention(attn_sink[l], qB, kB, vB, gB, qC, kC, vC, gC, yB, yC, dims)
        res = _merge(X, mod, ys[0], ys[1], gA, yB, yC, m, wb16[l], wo16[l], row(final_g), dims, tm_merge, last)
        if last:
            out = res.reshape(B, S, D_MODEL)
        else:
            X = res
    return out
```

```python
import functools
import math

import jax
import jax.numpy as jnp
from jax import lax
from jax.experimental import pallas as pl
from jax.experimental.pallas import tpu as pltpu

D_MODEL = 1024
HEAD_DIM = 128
N_HEADS = D_MODEL // HEAD_DIM
N_KV = N_HEADS // 4
GROUP = N_HEADS // N_KV
KV_W = N_KV * HEAD_DIM
LRU_BLOCK_W = 64
LRU_C = 8.0
CONV_W = 4
CONV_LEFT = 2
WINDOW = 128
GRID_W = 64
ROPE_THETA = 10000.0
EPS = 1e-6
NEG_INF = -1e30
LOG2E = 1.4426950408889634
Q_SCALE = HEAD_DIM ** -0.5 * LOG2E
GATE_CHUNK = 256
KEY_CHUNK = 512
LRU_BATCH = 4
VMEM_LIMIT = 56 * 1024 * 1024

F32 = jnp.float32
BF16 = jnp.bfloat16


def _cparams(sem):
    return pltpu.CompilerParams(dimension_semantics=sem, vmem_limit_bytes=VMEM_LIMIT)


def _sigmoid(x):
    return 0.5 * (jnp.tanh(0.5 * x) + 1.0)


def _mod_kernel(cc_ref, w_ref, b_ref, o_ref):
    cc = cc_ref[...]
    s = cc * _sigmoid(cc)
    o_ref[...] = jnp.dot(s, w_ref[...], preferred_element_type=F32,
                         precision=lax.Precision.HIGHEST) + b_ref[...]


def _modulation(cc, w_mod, b_mod):
    depth = w_mod.shape[0]
    rows = cc.shape[0]
    return pl.pallas_call(
        _mod_kernel,
        grid=(depth, 3),
        in_specs=[
            pl.BlockSpec((rows, D_MODEL), lambda l, j: (0, 0)),
            pl.BlockSpec((None, D_MODEL, D_MODEL), lambda l, j: (l, 0, j)),
            pl.BlockSpec((None, 1, D_MODEL), lambda l, j: (l, 0, j)),
        ],
        out_specs=pl.BlockSpec((None, rows, D_MODEL), lambda l, j: (l, 0, j)),
        out_shape=jax.ShapeDtypeStruct((depth, rows, 3 * D_MODEL), F32),
        compiler_params=_cparams(("arbitrary", "arbitrary")),
        name="modulation",
    )(cc, w_mod, b_mod.reshape(depth, 1, 3 * D_MODEL))


def _norm_mod(x, g, mod):
    ms = jnp.mean(x * x, axis=-1, keepdims=True)
    y = x * lax.rsqrt(ms + EPS) * g
    return y * (1.0 + mod[:, D_MODEL:2 * D_MODEL]) + mod[:, :D_MODEL]


def _pair_layout(w):
    lead = w.shape[:-1]
    nh = w.shape[-1] // HEAD_DIM
    return w.reshape(*lead, nh, 2, 2, HEAD_DIM // 4).swapaxes(-3, -2).reshape(*lead, nh * HEAD_DIM)


def _rope(yh, c, s):
    return yh * c + pltpu.roll(yh, HEAD_DIM // 2, 1) * s


def _head_norm(yh, g):
    avg = jnp.full((HEAD_DIM, HEAD_DIM), 1.0 / HEAD_DIM, BF16)
    ms = jnp.dot((yh * yh).astype(BF16), avg, preferred_element_type=F32)
    return yh * lax.rsqrt(ms + EPS) * g


def _inproj_kernel(specs, x_ref, mod_ref, g_ref, w_ref, c_ref, s_ref, qg_ref, kg_ref, *out_refs):
    h = _norm_mod(x_ref[...], g_ref[...], mod_ref[...]).astype(BF16)
    off = 0
    for (kind, width), o_ref in zip(specs, out_refs):
        y = jnp.dot(h, w_ref[:, off:off + width], preferred_element_type=F32)
        off += width
        if kind == "plain":
            o_ref[...] = y.astype(o_ref.dtype)
        elif kind == "value":
            for hh in range(width // HEAD_DIM):
                o_ref[:, 2 * hh * HEAD_DIM:(2 * hh + 1) * HEAD_DIM] = (
                    y[:, hh * HEAD_DIM:(hh + 1) * HEAD_DIM].astype(o_ref.dtype))
                o_ref[:, (2 * hh + 1) * HEAD_DIM:(2 * hh + 2) * HEAD_DIM] = jnp.ones(
                    (y.shape[0], HEAD_DIM), o_ref.dtype)
        elif kind == "silu":
            o_ref[...] = (y * _sigmoid(y)).astype(o_ref.dtype)
        elif kind == "sigmoid":
            o_ref[...] = _sigmoid(y).astype(o_ref.dtype)
        else:
            is_q = kind in ("rope_q", "normrope_q")
            norm = kind in ("normrope_q", "normrope_k")
            c, s = c_ref[...], s_ref[...]
            for hh in range(width // HEAD_DIM):
                sl = slice(hh * HEAD_DIM, (hh + 1) * HEAD_DIM)
                yh = y[:, sl]
                if norm:
                    yh = _head_norm(yh, qg_ref[...] if is_q else kg_ref[...])
                yh = _rope(yh, c, s)
                if is_q:
                    yh = yh * Q_SCALE
                o_ref[:, sl] = yh.astype(o_ref.dtype)


def _inproj(X, mod, g, w, tabs, qg, kg, specs, dtypes, dims, tm, name):
    B, S, L = dims
    n = X.shape[0]
    ts = S // tm
    nlat = B * ts
    wtot = w.shape[1]

    def mod_idx(i):
        return (jnp.where(i < nlat, i // ts, B), 0, 0)

    def tab_idx(i):
        return (jnp.where(i < nlat, i % ts, ts), 0)

    widths = [2 * wd if kind == "value" else wd for kind, wd in specs]
    out_shape = [jax.ShapeDtypeStruct((n, wd), dt) for wd, dt in zip(widths, dtypes)]
    out_specs = [pl.BlockSpec((tm, wd), lambda i: (i, 0)) for wd in widths]
    tab_spec = pl.BlockSpec((tm, HEAD_DIM), tab_idx)
    return pl.pallas_call(
        functools.partial(_inproj_kernel, specs),
        grid=(n // tm,),
        in_specs=[
            pl.BlockSpec((tm, D_MODEL), lambda i: (i, 0)),
            pl.BlockSpec((None, 1, 3 * D_MODEL), mod_idx),
            pl.BlockSpec((1, D_MODEL), lambda i: (0, 0)),
            pl.BlockSpec((D_MODEL, wtot), lambda i: (0, 0)),
            tab_spec, tab_spec,
            pl.BlockSpec((1, HEAD_DIM), lambda i: (0, 0)),
            pl.BlockSpec((1, HEAD_DIM), lambda i: (0, 0)),
        ],
        out_specs=out_specs,
        out_shape=out_shape,
        compiler_params=_cparams(("parallel",)),
        name=name,
    )(X, mod, g, w, tabs[0], tabs[1], qg, kg)


def _lru_kernel(reverse, nchunk, u_ref, up_ref, un_ref, h0_ref, cw_ref, cb_ref, wg_ref, ba_ref, bx_ref, lam_ref,
                *rest):
    y_ref, hout_ref, ext_s, a_s, b_s, h_s = rest[-6:]
    K, T, _ = u_ref.shape
    c = pl.program_id(1)
    j = (nchunk - 1 - c) if reverse else c
    has_prev = j > 0
    has_next = j < nchunk - 1

    lam = lam_ref[...]
    neg = -lam
    softplus = jnp.maximum(neg, 0.0) + jnp.log1p(jnp.exp(-jnp.abs(neg)))
    c1 = (-0.5 * LRU_C * LOG2E) * softplus
    hba = 0.5 * ba_ref[...]
    hbx = 0.5 * bx_ref[...]
    cw = 0.5 * cw_ref[...]
    cb = 0.5 * cb_ref[...]

    for kb in range(K):
        ext_s[kb, 0:8, :] = jnp.where(has_prev, up_ref[kb], 0.0)
        ext_s[kb, 8:8 + T, :] = u_ref[kb]
        ext_s[kb, 8 + T:16 + T, :] = jnp.where(has_next, un_ref[kb], 0.0)
        ext = ext_s[kb]
        hu = cb
        for k in range(CONV_W):
            shift = (CONV_LEFT - k) % (T + 16)
            tap = ext if shift == 0 else pltpu.roll(ext, shift, 0)
            hu = hu + tap[8:8 + T, :] * cw[k:k + 1, :]
        ub = hu.astype(BF16)
        for q in range(D_MODEL // GATE_CHUNK):
            sl = slice(q * GATE_CHUNK, (q + 1) * GATE_CHUNK)
            z = jnp.dot(ub[:, sl], wg_ref[q], preferred_element_type=F32)
            tr = jnp.tanh(z[:, :GATE_CHUNK] + hba[:, sl])
            ti = jnp.tanh(z[:, GATE_CHUNK:] + hbx[:, sl])
            a = jnp.exp2(c1[:, sl] * tr + c1[:, sl])
            x = 1.0 - a * a
            root = jnp.where(x > 0.0, x * lax.rsqrt(x), 0.0)
            b = root * ((ti + 1.0) * hu[:, sl])
            for n in range(GATE_CHUNK // HEAD_DIM):
                lanes = slice(n * HEAD_DIM, (n + 1) * HEAD_DIM)
                sub = q * (GATE_CHUNK // HEAD_DIM) + n
                a_s[kb, pl.ds(sub, T, stride=8), :] = a[:, lanes]
                b_s[kb, pl.ds(sub, T, stride=8), :] = b[:, lanes]

    @pl.when(c == 0)
    def _():
        h_s[...] = h0_ref[...]

    def body(t, hs):
        tt = (T - 1 - t) if reverse else t
        rows = pl.ds(pl.multiple_of(tt * 8, 8), 8)
        out = []
        for kb in range(K):
            h = a_s[kb, rows, :] * hs[kb] + b_s[kb, rows, :]
            b_s[kb, rows, :] = h
            out.append(h)
        return tuple(out)

    hs = lax.fori_loop(0, T, body, tuple(h_s[kb] for kb in range(K)), unroll=4)
    for kb in range(K):
        h_s[kb] = hs[kb]
        for n in range(D_MODEL // HEAD_DIM):
            y_ref[kb, :, n * HEAD_DIM:(n + 1) * HEAD_DIM] = b_s[kb, pl.ds(n, T, stride=8), :].astype(y_ref.dtype)

    @pl.when(c == nchunk - 1)
    def _():
        hout_ref[...] = h_s[...]


def _lru_call(u3, h0, y_prev, weights, K, T, seg0, reverse, name):
    rows = u3.shape[1]
    nchunk = rows // T
    t8 = T // 8
    blk0 = seg0 // K

    def chunk(c):
        return (nchunk - 1 - c) if reverse else c

    main = pl.BlockSpec((K, T, D_MODEL), lambda g, c: (blk0 + g, chunk(c), 0))
    prev = pl.BlockSpec((K, 8, D_MODEL), lambda g, c: (blk0 + g, jnp.maximum(chunk(c) * t8 - 1, 0), 0))
    nxt = pl.BlockSpec((K, 8, D_MODEL),
                       lambda g, c: (blk0 + g, jnp.minimum((chunk(c) + 1) * t8, rows // 8 - 1), 0))
    hspec = pl.BlockSpec((None, K, 8, HEAD_DIM), lambda g, c: (g, 0, 0, 0))
    fixed = lambda shape: pl.BlockSpec(shape, lambda g, c: (0,) * len(shape))
    in_specs = [main, prev, nxt, hspec, fixed((CONV_W, D_MODEL)), fixed((1, D_MODEL)),
                fixed((D_MODEL // GATE_CHUNK, GATE_CHUNK, 2 * GATE_CHUNK)),
                fixed((1, D_MODEL)), fixed((1, D_MODEL)), fixed((1, D_MODEL))]
    args = [u3, u3, u3, h0, *weights]
    aliases = {}
    if y_prev is not None:
        in_specs.append(pl.BlockSpec(memory_space=pl.ANY))
        args.append(y_prev)
        aliases = {len(args) - 1: 0}
    return pl.pallas_call(
        functools.partial(_lru_kernel, reverse, nchunk),
        grid=(h0.shape[0], nchunk),
        in_specs=in_specs,
        out_specs=[main, hspec],
        out_shape=[jax.ShapeDtypeStruct(u3.shape, BF16), jax.ShapeDtypeStruct(h0.shape, F32)],
        input_output_aliases=aliases,
        scratch_shapes=[
            pltpu.VMEM((K, T + 16, D_MODEL), F32),
            pltpu.VMEM((K, T * 8, HEAD_DIM), F32),
            pltpu.VMEM((K, T * 8, HEAD_DIM), F32),
            pltpu.VMEM((K, 8, HEAD_DIM), F32),
        ],
        compiler_params=_cparams(("arbitrary", "arbitrary")),
        name=name,
    )(*args)


def _lru(uA, weights, dims, reverse, name):
    B, S, L = dims
    n = uA.shape[0]
    K = LRU_BATCH
    assert B % K == 0 and n % S == 0 and (B * S // L) % K == 0
    assert D_MODEL == 8 * HEAD_DIM
    h0 = jnp.zeros((B // K, K, 8, HEAD_DIM), F32)
    y, h = _lru_call(uA.reshape(n // L, L, D_MODEL), h0, None, weights, K, L, B * S // L, reverse, name + "_ctx")
    y, _ = _lru_call(uA.reshape(n // S, S, D_MODEL), h, y.reshape(n // S, S, D_MODEL), weights, K, L, 0, reverse,
                     name)
    return y.reshape(n, D_MODEL)


def _gate_weights(wa, wx):
    per = GATE_CHUNK // LRU_BLOCK_W
    nchunks = wa.shape[0] // per
    eye = jnp.eye(per, dtype=wa.dtype)

    def bd(w):
        w4 = w.reshape(nchunks, per, LRU_BLOCK_W, LRU_BLOCK_W)
        return jnp.einsum("jkde,kl->jkdle", w4, eye).reshape(nchunks, GATE_CHUNK, GATE_CHUNK)

    return jnp.concatenate([bd(wa), bd(wx)], axis=-1).astype(BF16)


def _row_shift(mrun):
    mb = mrun.astype(BF16)
    return jnp.broadcast_to(mb.max(axis=-1, keepdims=True), mb.shape).astype(F32)


def _shifted_exp2(s, shift):
    blocks = [jnp.exp2(s[:, c * HEAD_DIM:(c + 1) * HEAD_DIM] - shift) for c in range(s.shape[1] // HEAD_DIM)]
    return jnp.concatenate(blocks, axis=1).astype(BF16)


def _lane_fold(x, acc, op):
    for c in range(x.shape[1] // HEAD_DIM):
        blk = x[:, c * HEAD_DIM:(c + 1) * HEAD_DIM]
        acc = blk if acc is None else op(acc, blk)
    return acc


def _softmax_pv(scores, values, sink):
    mrun = None
    for s in scores:
        mrun = _lane_fold(s, mrun, jnp.maximum)
    m = mrun.max(axis=-1, keepdims=True)
    if sink is not None:
        m = jnp.maximum(m, sink)
    o = None
    for s, v in zip(scores, values):
        d = jnp.dot(jnp.exp2(s - m).astype(BF16), v, preferred_element_type=F32)
        o = d if o is None else o + d
    l = o[:, HEAD_DIM:]
    if sink is not None:
        l = l + jnp.exp2(sink - m)
    return o[:, :HEAD_DIM] * (1.0 / l)


def _qk(q, k):
    return lax.dot_general(q, k, (((1,), (1,)), ((), ())), preferred_element_type=F32)


def _stack_heads(q_ref, g):
    return jnp.concatenate(
        [q_ref[:, (g * GROUP + h) * HEAD_DIM:(g * GROUP + h + 1) * HEAD_DIM] for h in range(GROUP)], axis=0)


def _sink_col(sink_ref, g, tq):
    return jnp.concatenate(
        [jnp.full((tq, 1), sink_ref[g * GROUP + h] * LOG2E, F32) for h in range(GROUP)], axis=0)


def _sink_lanes(sink_ref, g, tq):
    return jnp.concatenate(
        [jnp.full((tq, HEAD_DIM), sink_ref[g * GROUP + h] * LOG2E, F32) for h in range(GROUP)], axis=0)


def _store_heads(o_ref, gate_ref, o, g, tq):
    for h in range(GROUP):
        sl = slice((g * GROUP + h) * HEAD_DIM, (g * GROUP + h + 1) * HEAD_DIM)
        o_ref[:, sl] = (o[h * tq:(h + 1) * tq] * gate_ref[:, sl].astype(F32)).astype(o_ref.dtype)


def _kv_slices(g):
    return slice(g * HEAD_DIM, (g + 1) * HEAD_DIM), slice(2 * g * HEAD_DIM, 2 * (g + 1) * HEAD_DIM)


def _window_kernel(nlat, S, sink_ref, q_ref, kl_ref, vl_ref, kc_ref, vc_ref, gate_ref, o_ref, s_a, s_b, m_a, m_b):
    tq = q_ref.shape[0]
    L = kc_ref.shape[0]
    span = tq + 2 * WINDOW
    i = pl.program_id(1)

    def band_start(tile):
        return pl.multiple_of(jnp.clip(tile * tq - WINDOW, 0, S - span), HEAD_DIM)

    def step(write, read):
        if write is not None:
            start = band_start(i)
            qpos = i * tq + lax.broadcasted_iota(jnp.int32, (tq, span), 0)
            kpos = start + lax.broadcasted_iota(jnp.int32, (tq, span), 1)
            bias = jnp.where(jnp.abs(kpos - qpos) <= WINDOW, 0.0, NEG_INF)
            bias = jnp.concatenate([bias] * GROUP, axis=0)
            for g in range(N_KV):
                cs, _ = _kv_slices(g)
                qs = _stack_heads(q_ref, g)
                s_ctx = _qk(qs, kc_ref[:, cs])
                s_lat = _qk(qs, kl_ref[pl.ds(start, span), cs]) + bias
                write[0][g, :, :L] = s_ctx
                write[0][g, :, L:] = s_lat
                mrun = _lane_fold(s_lat, _lane_fold(s_ctx, None, jnp.maximum), jnp.maximum)
                write[1][g] = jnp.maximum(_row_shift(mrun), _sink_lanes(sink_ref, g, tq))
        if read is not None:
            start = band_start(i - 1)
            for g in range(N_KV):
                _, vs = _kv_slices(g)
                m = read[1][g]
                o = jnp.dot(_shifted_exp2(read[0][g, :, :L], m), vc_ref[:, vs], preferred_element_type=F32)
                o = o + jnp.dot(_shifted_exp2(read[0][g, :, L:], m), vl_ref[pl.ds(start, span), vs],
                                preferred_element_type=F32)
                l = o[:, HEAD_DIM:] + jnp.exp2(_sink_lanes(sink_ref, g, tq) - m)
                _store_heads(o_ref, gate_ref, o[:, :HEAD_DIM] * (1.0 / l), g, tq)

    a, b = (s_a, m_a), (s_b, m_b)
    odd = i % 2 == 1
    steady = jnp.logical_and(i > 0, i < nlat)

    @pl.when(i == 0)
    def _():
        step(a, None)

    @pl.when(jnp.logical_and(steady, odd))
    def _():
        step(b, a)

    @pl.when(jnp.logical_and(steady, jnp.logical_not(odd)))
    def _():
        step(a, b)

    @pl.when(i == nlat)
    def _():
        step(None, b)


def _window_attention(q, k, v, gate, sink, dims, tq, rows):
    B, S, L = dims
    nlat = S // tq
    assert nlat % 2 == 0
    M = GROUP * tq
    keys = L + tq + 2 * WINDOW
    lat_spec = lambda wd: pl.BlockSpec((S, wd), lambda b, i: (b, 0))
    ctx_spec = lambda wd: pl.BlockSpec((L, wd), lambda b, i: (B * S // L + b, 0))
    return pl.pallas_call(
        functools.partial(_window_kernel, nlat, S),
        grid=(B, nlat + 1),
        in_specs=[pl.BlockSpec(memory_space=pltpu.SMEM),
                  pl.BlockSpec((tq, D_MODEL), lambda b, i: (b * nlat + jnp.minimum(i, nlat - 1), 0)),
                  lat_spec(KV_W), lat_spec(2 * KV_W), ctx_spec(KV_W), ctx_spec(2 * KV_W),
                  pl.BlockSpec((tq, D_MODEL), lambda b, i: (b * nlat + jnp.maximum(i - 1, 0), 0))],
        out_specs=pl.BlockSpec((tq, D_MODEL), lambda b, i: (b * nlat + jnp.maximum(i - 1, 0), 0)),
        out_shape=jax.ShapeDtypeStruct((rows, D_MODEL), BF16),
        scratch_shapes=[pltpu.VMEM((N_KV, M, keys), F32)] * 2 + [pltpu.VMEM((N_KV, M, HEAD_DIM), F32)] * 2,
        compiler_params=_cparams(("arbitrary", "arbitrary")),
        name="attn_window",
    )(sink, q, k, v, k, v, gate)


def _dense_kernel(nlat, q_ref, kl_ref, vl_ref, kc_ref, vc_ref, gate_ref, o_ref, s_a, s_b, m_a, m_b):
    tq = q_ref.shape[0]
    L = kc_ref.shape[0]
    S = kl_ref.shape[0]
    i = pl.program_id(1)
    bounds = [(0, L)] + [(L + j0, L + j0 + KEY_CHUNK) for j0 in range(0, S, KEY_CHUNK)]

    def step(write, read):
        for g in range(N_KV):
            if write is None:
                break
            cs, _ = _kv_slices(g)
            qs = _stack_heads(q_ref, g)
            mrun = None
            for lo, hi in bounds:
                k = kc_ref[:, cs] if lo == 0 else kl_ref[lo - L:hi - L, cs]
                s = _qk(qs, k)
                write[0][g, :, lo:hi] = s
                mrun = _lane_fold(s, mrun, jnp.maximum)
            write[1][g] = _row_shift(mrun)
        for g in range(N_KV):
            if read is None:
                break
            _, vs = _kv_slices(g)
            m = read[1][g]
            o = None
            for lo, hi in bounds:
                v = vc_ref[:, vs] if lo == 0 else vl_ref[lo - L:hi - L, vs]
                d = jnp.dot(_shifted_exp2(read[0][g, :, lo:hi], m), v, preferred_element_type=F32)
                o = d if o is None else o + d
            _store_heads(o_ref, gate_ref, o[:, :HEAD_DIM] * (1.0 / o[:, HEAD_DIM:]), g, tq)

    a, b = (s_a, m_a), (s_b, m_b)
    odd = i % 2 == 1
    steady = jnp.logical_and(i > 0, i < nlat)

    @pl.when(i == 0)
    def _():
        step(a, None)

    @pl.when(jnp.logical_and(steady, odd))
    def _():
        step(b, a)

    @pl.when(jnp.logical_and(steady, jnp.logical_not(odd)))
    def _():
        step(a, b)

    @pl.when(i == nlat)
    def _():
        step(None, b)


def _dense_attention(q, k, v, gate, dims, tq, rows):
    B, S, L = dims
    nlat = S // tq
    assert nlat % 2 == 0
    one = pl.Buffered(1)
    return pl.pallas_call(
        functools.partial(_dense_kernel, nlat),
        grid=(B, nlat + 1),
        in_specs=[
            pl.BlockSpec((tq, D_MODEL), lambda b, i: (b * nlat + jnp.minimum(i, nlat - 1), 0)),
            pl.BlockSpec((S, KV_W), lambda b, i: (b, 0), pipeline_mode=one),
            pl.BlockSpec((S, 2 * KV_W), lambda b, i: (b, 0), pipeline_mode=one),
            pl.BlockSpec((L, KV_W), lambda b, i: (B * S // L + b, 0), pipeline_mode=one),
            pl.BlockSpec((L, 2 * KV_W), lambda b, i: (B * S // L + b, 0), pipeline_mode=one),
            pl.BlockSpec((tq, D_MODEL), lambda b, i: (b * nlat + jnp.maximum(i - 1, 0), 0)),
        ],
        out_specs=pl.BlockSpec((tq, D_MODEL), lambda b, i: (b * nlat + jnp.maximum(i - 1, 0), 0)),
        out_shape=jax.ShapeDtypeStruct((rows, D_MODEL), BF16),
        scratch_shapes=[pltpu.VMEM((N_KV, GROUP * tq, L + S), F32)] * 2
        + [pltpu.VMEM((N_KV, GROUP * tq, HEAD_DIM), F32)] * 2,
        compiler_params=_cparams(("arbitrary", "arbitrary")),
        name="attn_dense",
    )(q, k, v, k, v, gate)


def _ctx_kernel(sink_ref, qb_ref, kb_ref, vb_ref, gb_ref, qc_ref, kc_ref, vc_ref, gc_ref, yb_any, yc_any,
                ob_ref, oc_ref):
    del yb_any, yc_any
    tq = qb_ref.shape[0]
    for g in range(N_KV):
        cs, vs = _kv_slices(g)
        qs = _stack_heads(qb_ref, g)
        ob = _softmax_pv([_qk(qs, kb_ref[:, cs])], [vb_ref[:, vs]], _sink_col(sink_ref, g, tq))
        _store_heads(ob_ref, gb_ref, ob, g, tq)
        qs = _stack_heads(qc_ref, g)
        oc = _softmax_pv([_qk(qs, kc_ref[:, cs])], [vc_ref[:, vs]], None)
        _store_heads(oc_ref, gc_ref, oc, g, tq)


def _ctx_attention(sink, qB, kB, vB, gB, qC, kC, vC, gC, yB, yC, dims):
    B, S, L = dims
    c0 = B * S // L
    rows = lambda wd: pl.BlockSpec((L, wd), lambda b: (c0 + b, 0))
    anyspec = pl.BlockSpec(memory_space=pl.ANY)
    return pl.pallas_call(
        _ctx_kernel,
        grid=(B,),
        in_specs=[pl.BlockSpec(memory_space=pltpu.SMEM),
                  rows(D_MODEL), rows(KV_W), rows(2 * KV_W), rows(D_MODEL),
                  rows(D_MODEL), rows(KV_W), rows(2 * KV_W), rows(D_MODEL),
                  anyspec, anyspec],
        out_specs=[rows(D_MODEL), rows(D_MODEL)],
        out_shape=[jax.ShapeDtypeStruct(yB.shape, BF16), jax.ShapeDtypeStruct(yC.shape, BF16)],
        input_output_aliases={9: 0, 10: 1},
        compiler_params=_cparams(("arbitrary",)),
        name="attn_ctx",
    )(sink, qB, kB, vB, gB, qC, kC, vC, gC, yB, yC)


def _merge_kernel(final, x_ref, mod_ref, yf_ref, yr_ref, ga_ref, yb_ref, yc_ref, m_ref,
                  wb_ref, wo_ref, fg_ref, o_ref):
    ya = (yf_ref[...].astype(F32) + yr_ref[...].astype(F32)) * ga_ref[...].astype(F32)
    ts = (ya.astype(BF16), yb_ref[...], yc_ref[...])
    acc = 0.0
    for n in range(3):
        br = jnp.dot(ts[n], wb_ref[n], preferred_element_type=F32)
        acc = acc + m_ref[:, n * D_MODEL:(n + 1) * D_MODEL].astype(F32) * br
    y = jnp.dot(acc.astype(BF16), wo_ref[...], preferred_element_type=F32)
    xn = x_ref[...] + mod_ref[:, 2 * D_MODEL:] * y
    if final:
        xn = xn * lax.rsqrt(jnp.mean(xn * xn, axis=-1, keepdims=True) + EPS) * fg_ref[...]
    o_ref[...] = xn


def _merge(X, mod, yf, yr, ga, yb, yc, m, wb, wo, fg, dims, tm, final):
    B, S, L = dims
    n = B * S if final else X.shape[0]
    ts = S // tm
    nlat = B * ts

    def mod_idx(i):
        return (jnp.where(i < nlat, i // ts, B), 0, 0)

    rows = lambda wd: pl.BlockSpec((tm, wd), lambda i: (i, 0))
    return pl.pallas_call(
        functools.partial(_merge_kernel, final),
        grid=(n // tm,),
        in_specs=[
            rows(D_MODEL),
            pl.BlockSpec((None, 1, 3 * D_MODEL), mod_idx),
            rows(D_MODEL), rows(D_MODEL), rows(D_MODEL), rows(D_MODEL), rows(D_MODEL), rows(3 * D_MODEL),
            pl.BlockSpec((3, D_MODEL, D_MODEL), lambda i: (0, 0, 0), pipeline_mode=pl.Buffered(1)),
            pl.BlockSpec((D_MODEL, D_MODEL), lambda i: (0, 0), pipeline_mode=pl.Buffered(1)),
            pl.BlockSpec((1, D_MODEL), lambda i: (0, 0)),
        ],
        out_specs=rows(D_MODEL),
        out_shape=jax.ShapeDtypeStruct((n, D_MODEL), F32),
        compiler_params=_cparams(("parallel",)),
        name="merge_final" if final else "merge",
    )(X, mod, yf, yr, ga, yb, yc, m, wb, wo, fg)


def _rope_tables(S, tm):
    t = jnp.arange(S)
    P = HEAD_DIM // 4
    inv = ROPE_THETA ** (-jnp.arange(P, dtype=F32) / P)
    ang = jnp.stack([(t // GRID_W)[:, None] * inv, (t % GRID_W)[:, None] * inv], axis=1)
    cos = jnp.cos(ang).reshape(S, HEAD_DIM // 2)
    sin = jnp.sin(ang).reshape(S, HEAD_DIM // 2)
    c = jnp.concatenate([cos, cos], axis=1)
    s = jnp.concatenate([-sin, sin], axis=1)
    ident = jnp.zeros((tm, HEAD_DIM), F32)
    return jnp.concatenate([c, ident + 1.0]), jnp.concatenate([s, ident])


def kernel(x, c, ctx, c_ctx, norm_g, w_mod, b_mod, w_in, conv_w, conv_b, lru_wa, lru_ba, lru_wx, lru_bx,
           lru_lambda, attn_sink, q_norm_g, k_norm_g, w_branch, w_out, final_g):
    B, S, _ = x.shape
    L = ctx.shape[1]
    depth = w_in.shape[0]
    dims = (B, S, L)
    assert S % L == 0 and L % 128 == 0 and S % GRID_W == 0 and S >= 256 + 2 * WINDOW
    tm = math.gcd(1024, math.gcd(S, B * L))
    tm_merge = math.gcd(512, tm)
    tq_w = 256
    tq_g = 128

    X = jnp.concatenate([x.reshape(B * S, D_MODEL), ctx.reshape(B * L, D_MODEL)], axis=0)
    rows = -(-(B + 1) // 8) * 8
    cc = jnp.concatenate([c, c_ctx[None], jnp.zeros((rows - B - 1, D_MODEL), F32)], axis=0)
    mod_all = _modulation(cc, w_mod, b_mod).reshape(depth, rows, 1, 3 * D_MODEL)
    tabs = _rope_tables(S, tm)
    w_in16 = w_in.astype(BF16)
    cuts = (2 * D_MODEL, 3 * D_MODEL, 3 * D_MODEL + KV_W, 4 * D_MODEL + 2 * KV_W, 5 * D_MODEL + 2 * KV_W,
            5 * D_MODEL + 3 * KV_W)
    w_in16 = jnp.concatenate(
        [w_in16[..., :cuts[0]], _pair_layout(w_in16[..., cuts[0]:cuts[1]]), _pair_layout(w_in16[..., cuts[1]:cuts[2]]),
         w_in16[..., cuts[2]:cuts[3]], _pair_layout(w_in16[..., cuts[3]:cuts[4]]),
         _pair_layout(w_in16[..., cuts[4]:cuts[5]]), w_in16[..., cuts[5]:]], axis=-1)
    wb16 = w_branch.astype(BF16)
    wo16 = w_out.astype(BF16)
    row = lambda a: a.reshape(1, -1)

    a_specs = (("plain", D_MODEL), ("silu", D_MODEL))
    b_specs = (("rope_q", D_MODEL), ("rope_k", KV_W), ("value", KV_W), ("silu", D_MODEL))
    c_specs = (("normrope_q", D_MODEL), ("normrope_k", KV_W), ("value", KV_W), ("silu", D_MODEL))
    m_specs = (("sigmoid", 3 * D_MODEL),)
    a_w = 2 * D_MODEL
    b_w = 2 * D_MODEL + 2 * KV_W

    out = None
    for l in range(depth):
        last = l == depth - 1
        mod = mod_all[l]
        g = row(norm_g[l])
        qg, kg = _pair_layout(row(q_norm_g[l])), _pair_layout(row(k_norm_g[l]))
        w = w_in16[l]
        proj = functools.partial(_inproj, X, mod, g, tabs=tabs, qg=qg, kg=kg, dims=dims, tm=tm)
        uA, gA = proj(w=w[:, :a_w], specs=a_specs, dtypes=(F32, BF16), name="inproj_a")
        qB, kB, vB, gB = proj(w=w[:, a_w:a_w + b_w], specs=b_specs, dtypes=(BF16,) * 4, name="inproj_b")
        qC, kC, vC, gC = proj(w=w[:, a_w + b_w:a_w + 2 * b_w], specs=c_specs, dtypes=(BF16,) * 4,
                              name="inproj_c")
        (m,) = proj(w=w[:, a_w + 2 * b_w:], specs=m_specs, dtypes=(BF16,), name="inproj_m")

        ys = []
        for d, rev in enumerate((False, True)):
            wg = _gate_weights(lru_wa[l, d], lru_wx[l, d])
            weights = (conv_w[l], row(conv_b[l]), wg, row(lru_ba[l, d]), row(lru_bx[l, d]), row(lru_lambda[l, d]))
            ys.append(_lru(uA, weights, dims, rev, "lru_rev" if rev else "lru_fwd"))
        rows_out = B * S if last else X.shape[0]
        yB = _window_attention(qB, kB, vB, gB, attn_sink[l], dims, tq_w, rows_out)
        yC = _dense_attention(qC, kC, vC, gC, dims, tq_g, rows_out)
        if not last:
            yB, yC = _ctx_attention(attn_sink[l], qB, kB, vB, gB, qC, kC, vC, gC, yB, yC, dims)
        res = _merge(X, mod, ys[0], ys[1], gA, yB, yC, m, wb16[l], wo16[l], row(final_g), dims, tm_merge, last)
        if last:
            out = res.reshape(B, S, D_MODEL)
        else:
            X = res
    return out
```

```python
import functools
import math

import jax
import jax.numpy as jnp
from jax import lax
from jax.experimental import pallas as pl
from jax.experimental.pallas import tpu as pltpu

D_MODEL = 1024
HEAD_DIM = 128
N_HEADS = D_MODEL // HEAD_DIM
N_KV = N_HEADS // 4
GROUP = N_HEADS // N_KV
KV_W = N_KV * HEAD_DIM
LRU_BLOCK_W = 64
LRU_C = 8.0
CONV_W = 4
CONV_LEFT = 2
WINDOW = 128
GRID_W = 64
ROPE_THETA = 10000.0
EPS = 1e-6
NEG_INF = -1e30
LOG2E = 1.4426950408889634
Q_SCALE = HEAD_DIM ** -0.5 * LOG2E
GATE_CHUNK = 256
KEY_CHUNK = 512
LRU_BATCH = 4
VMEM_LIMIT = 56 * 1024 * 1024

F32 = jnp.float32
BF16 = jnp.bfloat16


def _cparams(sem):
    return pltpu.CompilerParams(dimension_semantics=sem, vmem_limit_bytes=VMEM_LIMIT)


def _sigmoid(x):
    return 0.5 * (jnp.tanh(0.5 * x) + 1.0)


def _mod_kernel(cc_ref, w_ref, b_ref, o_ref):
    cc = cc_ref[...]
    s = cc * _sigmoid(cc)
    o_ref[...] = jnp.dot(s, w_ref[...], preferred_element_type=F32,
                         precision=lax.Precision.HIGHEST) + b_ref[...]


def _modulation(cc, w_mod, b_mod):
    depth = w_mod.shape[0]
    rows = cc.shape[0]
    return pl.pallas_call(
        _mod_kernel,
        grid=(depth, 3),
        in_specs=[
            pl.BlockSpec((rows, D_MODEL), lambda l, j: (0, 0)),
            pl.BlockSpec((None, D_MODEL, D_MODEL), lambda l, j: (l, 0, j)),
            pl.BlockSpec((None, 1, D_MODEL), lambda l, j: (l, 0, j)),
        ],
        out_specs=pl.BlockSpec((None, rows, D_MODEL), lambda l, j: (l, 0, j)),
        out_shape=jax.ShapeDtypeStruct((depth, rows, 3 * D_MODEL), F32),
        compiler_params=_cparams(("arbitrary", "arbitrary")),
        name="modulation",
    )(cc, w_mod, b_mod.reshape(depth, 1, 3 * D_MODEL))


def _norm_mod(x, g, mod):
    ms = jnp.mean(x * x, axis=-1, keepdims=True)
    y = x * lax.rsqrt(ms + EPS) * g
    return y * (1.0 + mod[:, D_MODEL:2 * D_MODEL]) + mod[:, :D_MODEL]


def _pair_layout(w):
    lead = w.shape[:-1]
    nh = w.shape[-1] // HEAD_DIM
    return w.reshape(*lead, nh, 2, 2, HEAD_DIM // 4).swapaxes(-3, -2).reshape(*lead, nh * HEAD_DIM)


def _rope(yh, c, s):
    return yh * c + pltpu.roll(yh, HEAD_DIM // 2, 1) * s


def _head_rms_scale(y):
    two = 2 * HEAD_DIM
    rows = lax.broadcasted_iota(jnp.int32, (two, two), 0) // HEAD_DIM
    cols = lax.broadcasted_iota(jnp.int32, (two, two), 1) // HEAD_DIM
    avg = jnp.where(rows == cols, 1.0 / HEAD_DIM, 0.0).astype(BF16)
    sq = (y * y).astype(BF16)
    ms = [jnp.dot(sq[:, p * two:(p + 1) * two], avg, preferred_element_type=F32) for p in range(y.shape[1] // two)]
    return lax.rsqrt(jnp.concatenate(ms, axis=1) + EPS)


def _inproj_kernel(specs, x_ref, mod_ref, g_ref, w_ref, c_ref, s_ref, qg_ref, kg_ref, *out_refs):
    h = _norm_mod(x_ref[...], g_ref[...], mod_ref[...]).astype(BF16)
    off = 0
    for (kind, width), o_ref in zip(specs, out_refs):
        y = jnp.dot(h, w_ref[:, off:off + width], preferred_element_type=F32)
        off += width
        if kind == "plain":
            o_ref[...] = y.astype(o_ref.dtype)
        elif kind == "value":
            for hh in range(width // HEAD_DIM):
                o_ref[:, 2 * hh * HEAD_DIM:(2 * hh + 1) * HEAD_DIM] = (
                    y[:, hh * HEAD_DIM:(hh + 1) * HEAD_DIM].astype(o_ref.dtype))
                o_ref[:, (2 * hh + 1) * HEAD_DIM:(2 * hh + 2) * HEAD_DIM] = jnp.ones(
                    (y.shape[0], HEAD_DIM), o_ref.dtype)
        elif kind == "silu":
            o_ref[...] = (y * _sigmoid(y)).astype(o_ref.dtype)
        elif kind == "sigmoid":
            o_ref[...] = _sigmoid(y).astype(o_ref.dtype)
        else:
            is_q = kind in ("rope_q", "normrope_q")
            norm = kind in ("normrope_q", "normrope_k")
            c, s = c_ref[...], s_ref[...]
            scale = _head_rms_scale(y) if norm else None
            for hh in range(width // HEAD_DIM):
                sl = slice(hh * HEAD_DIM, (hh + 1) * HEAD_DIM)
                yh = y[:, sl]
                if norm:
                    yh = yh * scale[:, sl] * (qg_ref[...] if is_q else kg_ref[...])
                yh = _rope(yh, c, s)
                if is_q:
                    yh = yh * Q_SCALE
                o_ref[:, sl] = yh.astype(o_ref.dtype)


def _inproj(X, mod, g, w, tabs, qg, kg, specs, dtypes, dims, tm, name):
    B, S, L = dims
    n = X.shape[0]
    ts = S // tm
    nlat = B * ts
    wtot = w.shape[1]

    def mod_idx(i):
        return (jnp.where(i < nlat, i // ts, B), 0, 0)

    def tab_idx(i):
        return (jnp.where(i < nlat, i % ts, ts), 0)

    widths = [2 * wd if kind == "value" else wd for kind, wd in specs]
    out_shape = [jax.ShapeDtypeStruct((n, wd), dt) for wd, dt in zip(widths, dtypes)]
    out_specs = [pl.BlockSpec((tm, wd), lambda i: (i, 0)) for wd in widths]
    tab_spec = pl.BlockSpec((tm, HEAD_DIM), tab_idx)
    return pl.pallas_call(
        functools.partial(_inproj_kernel, specs),
        grid=(n // tm,),
        in_specs=[
            pl.BlockSpec((tm, D_MODEL), lambda i: (i, 0)),
            pl.BlockSpec((None, 1, 3 * D_MODEL), mod_idx),
            pl.BlockSpec((1, D_MODEL), lambda i: (0, 0)),
            pl.BlockSpec((D_MODEL, wtot), lambda i: (0, 0)),
            tab_spec, tab_spec,
            pl.BlockSpec((1, HEAD_DIM), lambda i: (0, 0)),
            pl.BlockSpec((1, HEAD_DIM), lambda i: (0, 0)),
        ],
        out_specs=out_specs,
        out_shape=out_shape,
        compiler_params=_cparams(("parallel",)),
        name=name,
    )(X, mod, g, w, tabs[0], tabs[1], qg, kg)


def _lru_kernel(reverse, nchunk, u_ref, up_ref, un_ref, h0_ref, cw_ref, cb_ref, wg_ref, ba_ref, bx_ref, lam_ref,
                *rest):
    y_ref, hout_ref, ext_s, hu_s, a_s, b_s, h_s = rest[-7:]
    K, T, _ = u_ref.shape
    c = pl.program_id(1)
    j = (nchunk - 1 - c) if reverse else c
    has_prev = j > 0
    has_next = j < nchunk - 1

    lam = lam_ref[...]
    neg = -lam
    softplus = jnp.maximum(neg, 0.0) + jnp.log1p(jnp.exp(-jnp.abs(neg)))
    c1 = (-0.5 * LRU_C * LOG2E) * softplus
    hba = 0.5 * ba_ref[...]
    hbx = 0.5 * bx_ref[...]
    cw = 0.5 * cw_ref[...]
    cb = 0.5 * cb_ref[...]

    nblk = D_MODEL // HEAD_DIM
    for kb in range(K):
        for n in range(nblk):
            lanes = slice(n * HEAD_DIM, (n + 1) * HEAD_DIM)
            ext_s[kb, pl.ds(n, 8, stride=8), :] = jnp.where(has_prev, up_ref[kb, :, lanes], 0.0)
            ext_s[kb, pl.ds(64 + n, T, stride=8), :] = u_ref[kb, :, lanes]
            ext_s[kb, pl.ds(64 + 8 * T + n, 8, stride=8), :] = jnp.where(has_next, un_ref[kb, :, lanes], 0.0)
        acc = None
        for k in range(CONV_W):
            tap = ext_s[kb, pl.ds((8 - CONV_LEFT + k) * 8, 8 * T), :].reshape(T, 8, HEAD_DIM)
            term = tap * cw[k]
            acc = term if acc is None else acc + term
        hu_s[kb] = (acc + cb).reshape(8 * T, HEAD_DIM)
        hu = jnp.concatenate([hu_s[kb, pl.ds(n, T, stride=8), :] for n in range(nblk)], axis=1)
        ub = hu.astype(BF16)
        for q in range(D_MODEL // GATE_CHUNK):
            sl = slice(q * GATE_CHUNK, (q + 1) * GATE_CHUNK)
            z = jnp.dot(ub[:, sl], wg_ref[q], preferred_element_type=F32)
            tr = jnp.tanh(z[:, :GATE_CHUNK] + hba[:, sl])
            ti = jnp.tanh(z[:, GATE_CHUNK:] + hbx[:, sl])
            a = jnp.exp2(c1[:, sl] * tr + c1[:, sl])
            x = 1.0 - a * a
            root = jnp.where(x > 0.0, x * lax.rsqrt(x), 0.0)
            b = root * ((ti + 1.0) * hu[:, sl])
            for n in range(GATE_CHUNK // HEAD_DIM):
                lanes = slice(n * HEAD_DIM, (n + 1) * HEAD_DIM)
                sub = q * (GATE_CHUNK // HEAD_DIM) + n
                a_s[kb, pl.ds(sub, T, stride=8), :] = a[:, lanes]
                b_s[kb, pl.ds(sub, T, stride=8), :] = b[:, lanes]

    @pl.when(c == 0)
    def _():
        h_s[...] = h0_ref[...]

    def body(t, hs):
        tt = (T - 1 - t) if reverse else t
        rows = pl.ds(pl.multiple_of(tt * 8, 8), 8)
        out = []
        for kb in range(K):
            h = a_s[kb, rows, :] * hs[kb] + b_s[kb, rows, :]
            b_s[kb, rows, :] = h
            out.append(h)
        return tuple(out)

    hs = lax.fori_loop(0, T, body, tuple(h_s[kb] for kb in range(K)), unroll=4)
    for kb in range(K):
        h_s[kb] = hs[kb]
        for n in range(D_MODEL // HEAD_DIM):
            y_ref[kb, :, n * HEAD_DIM:(n + 1) * HEAD_DIM] = b_s[kb, pl.ds(n, T, stride=8), :].astype(y_ref.dtype)

    @pl.when(c == nchunk - 1)
    def _():
        hout_ref[...] = h_s[...]


def _lru_call(u3, h0, y_prev, weights, K, T, seg0, reverse, name):
    rows = u3.shape[1]
    nchunk = rows // T
    t8 = T // 8
    blk0 = seg0 // K

    def chunk(c):
        return (nchunk - 1 - c) if reverse else c

    main = pl.BlockSpec((K, T, D_MODEL), lambda g, c: (blk0 + g, chunk(c), 0))
    prev = pl.BlockSpec((K, 8, D_MODEL), lambda g, c: (blk0 + g, jnp.maximum(chunk(c) * t8 - 1, 0), 0))
    nxt = pl.BlockSpec((K, 8, D_MODEL),
                       lambda g, c: (blk0 + g, jnp.minimum((chunk(c) + 1) * t8, rows // 8 - 1), 0))
    hspec = pl.BlockSpec((None, K, 8, HEAD_DIM), lambda g, c: (g, 0, 0, 0))
    fixed = lambda shape: pl.BlockSpec(shape, lambda g, c: (0,) * len(shape))
    in_specs = [main, prev, nxt, hspec, fixed((CONV_W, 8, HEAD_DIM)), fixed((8, HEAD_DIM)),
                fixed((D_MODEL // GATE_CHUNK, GATE_CHUNK, 2 * GATE_CHUNK)),
                fixed((1, D_MODEL)), fixed((1, D_MODEL)), fixed((1, D_MODEL))]
    args = [u3, u3, u3, h0, *weights]
    aliases = {}
    if y_prev is not None:
        in_specs.append(pl.BlockSpec(memory_space=pl.ANY))
        args.append(y_prev)
        aliases = {len(args) - 1: 0}
    return pl.pallas_call(
        functools.partial(_lru_kernel, reverse, nchunk),
        grid=(h0.shape[0], nchunk),
        in_specs=in_specs,
        out_specs=[main, hspec],
        out_shape=[jax.ShapeDtypeStruct(u3.shape, BF16), jax.ShapeDtypeStruct(h0.shape, F32)],
        input_output_aliases=aliases,
        scratch_shapes=[
            pltpu.VMEM((K, (T + 16) * 8, HEAD_DIM), F32),
            pltpu.VMEM((K, T * 8, HEAD_DIM), F32),
            pltpu.VMEM((K, T * 8, HEAD_DIM), F32),
            pltpu.VMEM((K, T * 8, HEAD_DIM), F32),
            pltpu.VMEM((K, 8, HEAD_DIM), F32),
        ],
        compiler_params=_cparams(("arbitrary", "arbitrary")),
        name=name,
    )(*args)


def _lru(uA, weights, dims, reverse, name):
    B, S, L = dims
    n = uA.shape[0]
    K = LRU_BATCH
    assert B % K == 0 and n % S == 0 and (B * S // L) % K == 0
    assert D_MODEL == 8 * HEAD_DIM
    h0 = jnp.zeros((B // K, K, 8, HEAD_DIM), F32)
    y, h = _lru_call(uA.reshape(n // L, L, D_MODEL), h0, None, weights, K, L, B * S // L, reverse, name + "_ctx")
    y, _ = _lru_call(uA.reshape(n // S, S, D_MODEL), h, y.reshape(n // S, S, D_MODEL), weights, K, L, 0, reverse,
                     name)
    return y.reshape(n, D_MODEL)


def _gate_weights(wa, wx):
    per = GATE_CHUNK // LRU_BLOCK_W
    nchunks = wa.shape[0] // per
    eye = jnp.eye(per, dtype=wa.dtype)

    def bd(w):
        w4 = w.reshape(nchunks, per, LRU_BLOCK_W, LRU_BLOCK_W)
        return jnp.einsum("jkde,kl->jkdle", w4, eye).reshape(nchunks, GATE_CHUNK, GATE_CHUNK)

    return jnp.concatenate([bd(wa), bd(wx)], axis=-1).astype(BF16)


def _row_shift(mrun):
    mb = mrun.astype(BF16)
    return jnp.broadcast_to(mb.max(axis=-1, keepdims=True), mb.shape).astype(F32)


def _shifted_exp2(s, shift):
    blocks = [jnp.exp2(s[:, c * HEAD_DIM:(c + 1) * HEAD_DIM] - shift) for c in range(s.shape[1] // HEAD_DIM)]
    return jnp.concatenate(blocks, axis=1).astype(BF16)


def _lane_fold(x, acc, op):
    for c in range(x.shape[1] // HEAD_DIM):
        blk = x[:, c * HEAD_DIM:(c + 1) * HEAD_DIM]
        acc = blk if acc is None else op(acc, blk)
    return acc


def _softmax_pv(scores, values, sink):
    mrun = None
    for s in scores:
        mrun = _lane_fold(s, mrun, jnp.maximum)
    m = mrun.max(axis=-1, keepdims=True)
    if sink is not None:
        m = jnp.maximum(m, sink)
    o = None
    for s, v in zip(scores, values):
        d = jnp.dot(jnp.exp2(s - m).astype(BF16), v, preferred_element_type=F32)
        o = d if o is None else o + d
    l = o[:, HEAD_DIM:]
    if sink is not None:
        l = l + jnp.exp2(sink - m)
    return o[:, :HEAD_DIM] * (1.0 / l)


def _qk(q, k):
    return lax.dot_general(q, k, (((1,), (1,)), ((), ())), preferred_element_type=F32)


def _stack_heads(q_ref, g):
    return jnp.concatenate(
        [q_ref[:, (g * GROUP + h) * HEAD_DIM:(g * GROUP + h + 1) * HEAD_DIM] for h in range(GROUP)], axis=0)


def _sink_col(sink_ref, g, tq):
    return jnp.concatenate(
        [jnp.full((tq, 1), sink_ref[g * GROUP + h] * LOG2E, F32) for h in range(GROUP)], axis=0)


def _sink_lanes(sink_ref, g, tq):
    return jnp.concatenate(
        [jnp.full((tq, HEAD_DIM), sink_ref[g * GROUP + h] * LOG2E, F32) for h in range(GROUP)], axis=0)


def _store_heads(o_ref, gate_ref, o, g, tq):
    for h in range(GROUP):
        sl = slice((g * GROUP + h) * HEAD_DIM, (g * GROUP + h + 1) * HEAD_DIM)
        o_ref[:, sl] = (o[h * tq:(h + 1) * tq] * gate_ref[:, sl].astype(F32)).astype(o_ref.dtype)


def _kv_slices(g):
    return slice(g * HEAD_DIM, (g + 1) * HEAD_DIM), slice(2 * g * HEAD_DIM, 2 * (g + 1) * HEAD_DIM)


def _window_kernel(nlat, S, sink_ref, q_ref, kl_ref, vl_ref, kc_ref, vc_ref, gate_ref, o_ref, s_a, s_b, m_a, m_b):
    tq = q_ref.shape[0]
    L = kc_ref.shape[0]
    span = tq + 2 * WINDOW
    i = pl.program_id(1)

    def band_start(tile):
        return pl.multiple_of(jnp.clip(tile * tq - WINDOW, 0, S - span), HEAD_DIM)

    def step(write, read):
        if write is not None:
            start = band_start(i)
            qpos = i * tq + lax.broadcasted_iota(jnp.int32, (tq, span), 0)
            kpos = start + lax.broadcasted_iota(jnp.int32, (tq, span), 1)
            bias = jnp.where(jnp.abs(kpos - qpos) <= WINDOW, 0.0, NEG_INF)
            bias = jnp.concatenate([bias] * GROUP, axis=0)
            for g in range(N_KV):
                cs, _ = _kv_slices(g)
                qs = _stack_heads(q_ref, g)
                s_ctx = _qk(qs, kc_ref[:, cs])
                s_lat = _qk(qs, kl_ref[pl.ds(start, span), cs]) + bias
                write[0][g, :, :L] = s_ctx
                write[0][g, :, L:] = s_lat
                mrun = _lane_fold(s_lat, _lane_fold(s_ctx, None, jnp.maximum), jnp.maximum)
                write[1][g] = jnp.maximum(_row_shift(mrun), _sink_lanes(sink_ref, g, tq))
        if read is not None:
            start = band_start(i - 1)
            for g in range(N_KV):
                _, vs = _kv_slices(g)
                m = read[1][g]
                o = jnp.dot(_shifted_exp2(read[0][g, :, :L], m), vc_ref[:, vs], preferred_element_type=F32)
                o = o + jnp.dot(_shifted_exp2(read[0][g, :, L:], m), vl_ref[pl.ds(start, span), vs],
                                preferred_element_type=F32)
                l = o[:, HEAD_DIM:] + jnp.exp2(_sink_lanes(sink_ref, g, tq) - m)
                _store_heads(o_ref, gate_ref, o[:, :HEAD_DIM] * (1.0 / l), g, tq)

    a, b = (s_a, m_a), (s_b, m_b)
    odd = i % 2 == 1
    steady = jnp.logical_and(i > 0, i < nlat)

    @pl.when(i == 0)
    def _():
        step(a, None)

    @pl.when(jnp.logical_and(steady, odd))
    def _():
        step(b, a)

    @pl.when(jnp.logical_and(steady, jnp.logical_not(odd)))
    def _():
        step(a, b)

    @pl.when(i == nlat)
    def _():
        step(None, b)


def _window_attention(q, k, v, gate, sink, dims, tq, rows):
    B, S, L = dims
    nlat = S // tq
    assert nlat % 2 == 0
    M = GROUP * tq
    keys = L + tq + 2 * WINDOW
    lat_spec = lambda wd: pl.BlockSpec((S, wd), lambda b, i: (b, 0))
    ctx_spec = lambda wd: pl.BlockSpec((L, wd), lambda b, i: (B * S // L + b, 0))
    return pl.pallas_call(
        functools.partial(_window_kernel, nlat, S),
        grid=(B, nlat + 1),
        in_specs=[pl.BlockSpec(memory_space=pltpu.SMEM),
                  pl.BlockSpec((tq, D_MODEL), lambda b, i: (b * nlat + jnp.minimum(i, nlat - 1), 0)),
                  lat_spec(KV_W), lat_spec(2 * KV_W), ctx_spec(KV_W), ctx_spec(2 * KV_W),
                  pl.BlockSpec((tq, D_MODEL), lambda b, i: (b * nlat + jnp.maximum(i - 1, 0), 0))],
        out_specs=pl.BlockSpec((tq, D_MODEL), lambda b, i: (b * nlat + jnp.maximum(i - 1, 0), 0)),
        out_shape=jax.ShapeDtypeStruct((rows, D_MODEL), BF16),
        scratch_shapes=[pltpu.VMEM((N_KV, M, keys), F32)] * 2 + [pltpu.VMEM((N_KV, M, HEAD_DIM), F32)] * 2,
        compiler_params=_cparams(("arbitrary", "arbitrary")),
        name="attn_window",
    )(sink, q, k, v, k, v, gate)


def _dense_kernel(nlat, q_ref, kl_ref, vl_ref, kc_ref, vc_ref, gate_ref, o_ref, s_a, s_b, m_a, m_b):
    tq = q_ref.shape[0]
    L = kc_ref.shape[0]
    S = kl_ref.shape[0]
    i = pl.program_id(1)
    bounds = [(0, L)] + [(L + j0, L + j0 + KEY_CHUNK) for j0 in range(0, S, KEY_CHUNK)]

    def step(write, read):
        for g in range(N_KV):
            if write is None:
                break
            cs, _ = _kv_slices(g)
            qs = _stack_heads(q_ref, g)
            mrun = None
            for lo, hi in bounds:
                k = kc_ref[:, cs] if lo == 0 else kl_ref[lo - L:hi - L, cs]
                s = _qk(qs, k)
                write[0][g, :, lo:hi] = s
                mrun = _lane_fold(s, mrun, jnp.maximum)
            write[1][g] = _row_shift(mrun)
        for g in range(N_KV):
            if read is None:
                break
            _, vs = _kv_slices(g)
            m = read[1][g]
            o = None
            for lo, hi in bounds:
                v = vc_ref[:, vs] if lo == 0 else vl_ref[lo - L:hi - L, vs]
                d = jnp.dot(_shifted_exp2(read[0][g, :, lo:hi], m), v, preferred_element_type=F32)
                o = d if o is None else o + d
            _store_heads(o_ref, gate_ref, o[:, :HEAD_DIM] * (1.0 / o[:, HEAD_DIM:]), g, tq)

    a, b = (s_a, m_a), (s_b, m_b)
    odd = i % 2 == 1
    steady = jnp.logical_and(i > 0, i < nlat)

    @pl.when(i == 0)
    def _():
        step(a, None)

    @pl.when(jnp.logical_and(steady, odd))
    def _():
        step(b, a)

    @pl.when(jnp.logical_and(steady, jnp.logical_not(odd)))
    def _():
        step(a, b)

    @pl.when(i == nlat)
    def _():
        step(None, b)


def _dense_attention(q, k, v, gate, dims, tq, rows):
    B, S, L = dims
    nlat = S // tq
    assert nlat % 2 == 0
    one = pl.Buffered(1)
    return pl.pallas_call(
        functools.partial(_dense_kernel, nlat),
        grid=(B, nlat + 1),
        in_specs=[
            pl.BlockSpec((tq, D_MODEL), lambda b, i: (b * nlat + jnp.minimum(i, nlat - 1), 0)),
            pl.BlockSpec((S, KV_W), lambda b, i: (b, 0), pipeline_mode=one),
            pl.BlockSpec((S, 2 * KV_W), lambda b, i: (b, 0), pipeline_mode=one),
            pl.BlockSpec((L, KV_W), lambda b, i: (B * S // L + b, 0), pipeline_mode=one),
            pl.BlockSpec((L, 2 * KV_W), lambda b, i: (B * S // L + b, 0), pipeline_mode=one),
            pl.BlockSpec((tq, D_MODEL), lambda b, i: (b * nlat + jnp.maximum(i - 1, 0), 0)),
        ],
        out_specs=pl.BlockSpec((tq, D_MODEL), lambda b, i: (b * nlat + jnp.maximum(i - 1, 0), 0)),
        out_shape=jax.ShapeDtypeStruct((rows, D_MODEL), BF16),
        scratch_shapes=[pltpu.VMEM((N_KV, GROUP * tq, L + S), F32)] * 2
        + [pltpu.VMEM((N_KV, GROUP * tq, HEAD_DIM), F32)] * 2,
        compiler_params=_cparams(("arbitrary", "arbitrary")),
        name="attn_dense",
    )(q, k, v, k, v, gate)


def _ctx_kernel(sink_ref, qb_ref, kb_ref, vb_ref, gb_ref, qc_ref, kc_ref, vc_ref, gc_ref, yb_any, yc_any,
                ob_ref, oc_ref):
    del yb_any, yc_any
    tq = qb_ref.shape[0]
    for g in range(N_KV):
        cs, vs = _kv_slices(g)
        qs = _stack_heads(qb_ref, g)
        ob = _softmax_pv([_qk(qs, kb_ref[:, cs])], [vb_ref[:, vs]], _sink_col(sink_ref, g, tq))
        _store_heads(ob_ref, gb_ref, ob, g, tq)
        qs = _stack_heads(qc_ref, g)
        oc = _softmax_pv([_qk(qs, kc_ref[:, cs])], [vc_ref[:, vs]], None)
        _store_heads(oc_ref, gc_ref, oc, g, tq)


def _ctx_attention(sink, qB, kB, vB, gB, qC, kC, vC, gC, yB, yC, dims):
    B, S, L = dims
    c0 = B * S // L
    rows = lambda wd: pl.BlockSpec((L, wd), lambda b: (c0 + b, 0))
    anyspec = pl.BlockSpec(memory_space=pl.ANY)
    return pl.pallas_call(
        _ctx_kernel,
        grid=(B,),
        in_specs=[pl.BlockSpec(memory_space=pltpu.SMEM),
                  rows(D_MODEL), rows(KV_W), rows(2 * KV_W), rows(D_MODEL),
                  rows(D_MODEL), rows(KV_W), rows(2 * KV_W), rows(D_MODEL),
                  anyspec, anyspec],
        out_specs=[rows(D_MODEL), rows(D_MODEL)],
        out_shape=[jax.ShapeDtypeStruct(yB.shape, BF16), jax.ShapeDtypeStruct(yC.shape, BF16)],
        input_output_aliases={9: 0, 10: 1},
        compiler_params=_cparams(("arbitrary",)),
        name="attn_ctx",
    )(sink, qB, kB, vB, gB, qC, kC, vC, gC, yB, yC)


def _merge_kernel(final, x_ref, mod_ref, yf_ref, yr_ref, ga_ref, yb_ref, yc_ref, m_ref,
                  wb_ref, wo_ref, fg_ref, o_ref):
    ya = (yf_ref[...].astype(F32) + yr_ref[...].astype(F32)) * ga_ref[...].astype(F32)
    ts = (ya.astype(BF16), yb_ref[...], yc_ref[...])
    acc = 0.0
    for n in range(3):
        br = jnp.dot(ts[n], wb_ref[n], preferred_element_type=F32)
        acc = acc + m_ref[:, n * D_MODEL:(n + 1) * D_MODEL].astype(F32) * br
    y = jnp.dot(acc.astype(BF16), wo_ref[...], preferred_element_type=F32)
    xn = x_ref[...] + mod_ref[:, 2 * D_MODEL:] * y
    if final:
        xn = xn * lax.rsqrt(jnp.mean(xn * xn, axis=-1, keepdims=True) + EPS) * fg_ref[...]
    o_ref[...] = xn


def _merge(X, mod, yf, yr, ga, yb, yc, m, wb, wo, fg, dims, tm, final):
    B, S, L = dims
    n = B * S if final else X.shape[0]
    ts = S // tm
    nlat = B * ts

    def mod_idx(i):
        return (jnp.where(i < nlat, i // ts, B), 0, 0)

    rows = lambda wd: pl.BlockSpec((tm, wd), lambda i: (i, 0))
    return pl.pallas_call(
        functools.partial(_merge_kernel, final),
        grid=(n // tm,),
        in_specs=[
            rows(D_MODEL),
            pl.BlockSpec((None, 1, 3 * D_MODEL), mod_idx),
            rows(D_MODEL), rows(D_MODEL), rows(D_MODEL), rows(D_MODEL), rows(D_MODEL), rows(3 * D_MODEL),
            pl.BlockSpec((3, D_MODEL, D_MODEL), lambda i: (0, 0, 0), pipeline_mode=pl.Buffered(1)),
            pl.BlockSpec((D_MODEL, D_MODEL), lambda i: (0, 0), pipeline_mode=pl.Buffered(1)),
            pl.BlockSpec((1, D_MODEL), lambda i: (0, 0)),
        ],
        out_specs=rows(D_MODEL),
        out_shape=jax.ShapeDtypeStruct((n, D_MODEL), F32),
        compiler_params=_cparams(("parallel",)),
        name="merge_final" if final else "merge",
    )(X, mod, yf, yr, ga, yb, yc, m, wb, wo, fg)


def _rope_tables(S, tm):
    t = jnp.arange(S)
    P = HEAD_DIM // 4
    inv = ROPE_THETA ** (-jnp.arange(P, dtype=F32) / P)
    ang = jnp.stack([(t // GRID_W)[:, None] * inv, (t % GRID_W)[:, None] * inv], axis=1)
    cos = jnp.cos(ang).reshape(S, HEAD_DIM // 2)
    sin = jnp.sin(ang).reshape(S, HEAD_DIM // 2)
    c = jnp.concatenate([cos, cos], axis=1)
    s = jnp.concatenate([-sin, sin], axis=1)
    ident = jnp.zeros((tm, HEAD_DIM), F32)
    return jnp.concatenate([c, ident + 1.0]), jnp.concatenate([s, ident])


def kernel(x, c, ctx, c_ctx, norm_g, w_mod, b_mod, w_in, conv_w, conv_b, lru_wa, lru_ba, lru_wx, lru_bx,
           lru_lambda, attn_sink, q_norm_g, k_norm_g, w_branch, w_out, final_g):
    B, S, _ = x.shape
    L = ctx.shape[1]
    depth = w_in.shape[0]
    dims = (B, S, L)
    assert S % L == 0 and L % 128 == 0 and S % GRID_W == 0 and S >= 256 + 2 * WINDOW
    tm = math.gcd(1024, math.gcd(S, B * L))
    tm_merge = math.gcd(512, tm)
    tq_w = 256
    tq_g = 128

    X = jnp.concatenate([x.reshape(B * S, D_MODEL), ctx.reshape(B * L, D_MODEL)], axis=0)
    rows = -(-(B + 1) // 8) * 8
    cc = jnp.concatenate([c, c_ctx[None], jnp.zeros((rows - B - 1, D_MODEL), F32)], axis=0)
    mod_all = _modulation(cc, w_mod, b_mod).reshape(depth, rows, 1, 3 * D_MODEL)
    tabs = _rope_tables(S, tm)
    w_in16 = w_in.astype(BF16)
    cuts = (2 * D_MODEL, 3 * D_MODEL, 3 * D_MODEL + KV_W, 4 * D_MODEL + 2 * KV_W, 5 * D_MODEL + 2 * KV_W,
            5 * D_MODEL + 3 * KV_W)
    w_in16 = jnp.concatenate(
        [w_in16[..., :cuts[0]], _pair_layout(w_in16[..., cuts[0]:cuts[1]]), _pair_layout(w_in16[..., cuts[1]:cuts[2]]),
         w_in16[..., cuts[2]:cuts[3]], _pair_layout(w_in16[..., cuts[3]:cuts[4]]),
         _pair_layout(w_in16[..., cuts[4]:cuts[5]]), w_in16[..., cuts[5]:]], axis=-1)
    wb16 = w_branch.astype(BF16)
    wo16 = w_out.astype(BF16)
    row = lambda a: a.reshape(1, -1)

    a_specs = (("plain", D_MODEL), ("silu", D_MODEL))
    b_specs = (("rope_q", D_MODEL), ("rope_k", KV_W), ("value", KV_W), ("silu", D_MODEL))
    c_specs = (("normrope_q", D_MODEL), ("normrope_k", KV_W), ("value", KV_W), ("silu", D_MODEL))
    m_specs = (("sigmoid", 3 * D_MODEL),)
    a_w = 2 * D_MODEL
    b_w = 2 * D_MODEL + 2 * KV_W

    out = None
    for l in range(depth):
        last = l == depth - 1
        mod = mod_all[l]
        g = row(norm_g[l])
        qg, kg = _pair_layout(row(q_norm_g[l])), _pair_layout(row(k_norm_g[l]))
        w = w_in16[l]
        proj = functools.partial(_inproj, X, mod, g, tabs=tabs, qg=qg, kg=kg, dims=dims, tm=tm)
        uA, gA = proj(w=w[:, :a_w], specs=a_specs, dtypes=(F32, BF16), name="inproj_a")
        qB, kB, vB, gB = proj(w=w[:, a_w:a_w + b_w], specs=b_specs, dtypes=(BF16,) * 4, name="inproj_b")
        qC, kC, vC, gC = proj(w=w[:, a_w + b_w:a_w + 2 * b_w], specs=c_specs, dtypes=(BF16,) * 4,
                              name="inproj_c")
        (m,) = proj(w=w[:, a_w + 2 * b_w:], specs=m_specs, dtypes=(BF16,), name="inproj_m")

        ys = []
        for d, rev in enumerate((False, True)):
            wg = _gate_weights(lru_wa[l, d], lru_wx[l, d])
            weights = (conv_w[l].reshape(CONV_W, 8, HEAD_DIM), conv_b[l].reshape(8, HEAD_DIM), wg,
                       row(lru_ba[l, d]), row(lru_bx[l, d]), row(lru_lambda[l, d]))
            ys.append(_lru(uA, weights, dims, rev, "lru_rev" if rev else "lru_fwd"))
        rows_out = B * S if last else X.shape[0]
        yB = _window_attention(qB, kB, vB, gB, attn_sink[l], dims, tq_w, rows_out)
        yC = _dense_attention(qC, kC, vC, gC, dims, tq_g, rows_out)
        if not last:
            yB, yC = _ctx_attention(attn_sink[l], qB, kB, vB, gB, qC, kC, vC, gC, yB, yC, dims)
        res = _merge(X, mod, ys[0], ys[1], gA, yB, yC, m, wb16[l], wo16[l], row(final_g), dims, tm_merge, last)
        if last:
            out = res.reshape(B, S, D_MODEL)
        else:
            X = res
    return out
```

```python
import functools
import math

import jax
import jax.numpy as jnp
from jax import lax
from jax.experimental import pallas as pl
from jax.experimental.pallas import tpu as pltpu

D_MODEL = 1024
HEAD_DIM = 128
N_HEADS = D_MODEL // HEAD_DIM
N_KV = N_HEADS // 4
GROUP = N_HEADS // N_KV
KV_W = N_KV * HEAD_DIM
LRU_BLOCK_W = 64
LRU_C = 8.0
CONV_W = 4
CONV_LEFT = 2
WINDOW = 128
GRID_W = 64
ROPE_THETA = 10000.0
EPS = 1e-6
NEG_INF = -1e30
LOG2E = 1.4426950408889634
Q_SCALE = HEAD_DIM ** -0.5 * LOG2E
GATE_CHUNK = 256
KEY_CHUNK = 512
LRU_BATCH = 4
VMEM_LIMIT = 56 * 1024 * 1024

F32 = jnp.float32
BF16 = jnp.bfloat16


def _cparams(sem):
    return pltpu.CompilerParams(dimension_semantics=sem, vmem_limit_bytes=VMEM_LIMIT)


def _sigmoid(x):
    return 0.5 * (jnp.tanh(0.5 * x) + 1.0)


def _mod_kernel(cc_ref, w_ref, b_ref, o_ref):
    cc = cc_ref[...]
    s = cc * _sigmoid(cc)
    o_ref[...] = jnp.dot(s, w_ref[...], preferred_element_type=F32,
                         precision=lax.Precision.HIGHEST) + b_ref[...]


def _modulation(cc, w_mod, b_mod):
    depth = w_mod.shape[0]
    rows = cc.shape[0]
    return pl.pallas_call(
        _mod_kernel,
        grid=(depth, 3),
        in_specs=[
            pl.BlockSpec((rows, D_MODEL), lambda l, j: (0, 0)),
            pl.BlockSpec((None, D_MODEL, D_MODEL), lambda l, j: (l, 0, j)),
            pl.BlockSpec((None, 1, D_MODEL), lambda l, j: (l, 0, j)),
        ],
        out_specs=pl.BlockSpec((None, rows, D_MODEL), lambda l, j: (l, 0, j)),
        out_shape=jax.ShapeDtypeStruct((depth, rows, 3 * D_MODEL), F32),
        compiler_params=_cparams(("arbitrary", "arbitrary")),
        name="modulation",
    )(cc, w_mod, b_mod.reshape(depth, 1, 3 * D_MODEL))


def _norm_mod(x, g, mod):
    ms = jnp.mean(x * x, axis=-1, keepdims=True)
    y = x * lax.rsqrt(ms + EPS) * g
    return y * (1.0 + mod[:, D_MODEL:2 * D_MODEL]) + mod[:, :D_MODEL]


def _pair_layout(w):
    lead = w.shape[:-1]
    nh = w.shape[-1] // HEAD_DIM
    return w.reshape(*lead, nh, 2, 2, HEAD_DIM // 4).swapaxes(-3, -2).reshape(*lead, nh * HEAD_DIM)


def _rope(yh, c, s):
    return yh * c + pltpu.roll(yh, HEAD_DIM // 2, 1) * s


def _head_rms_scale(y):
    two = 2 * HEAD_DIM
    rows = lax.broadcasted_iota(jnp.int32, (two, two), 0) // HEAD_DIM
    cols = lax.broadcasted_iota(jnp.int32, (two, two), 1) // HEAD_DIM
    avg = jnp.where(rows == cols, 1.0 / HEAD_DIM, 0.0).astype(BF16)
    sq = (y * y).astype(BF16)
    ms = [jnp.dot(sq[:, p * two:(p + 1) * two], avg, preferred_element_type=F32) for p in range(y.shape[1] // two)]
    return lax.rsqrt(jnp.concatenate(ms, axis=1) + EPS)


def _inproj_kernel(specs, x_ref, mod_ref, g_ref, w_ref, c_ref, s_ref, qg_ref, kg_ref, *out_refs):
    h = _norm_mod(x_ref[...], g_ref[...], mod_ref[...]).astype(BF16)
    off = 0
    for (kind, width), o_ref in zip(specs, out_refs):
        y = jnp.dot(h, w_ref[:, off:off + width], preferred_element_type=F32)
        off += width
        if kind == "plain":
            o_ref[...] = y.astype(o_ref.dtype)
        elif kind == "value":
            for hh in range(width // HEAD_DIM):
                o_ref[:, 2 * hh * HEAD_DIM:(2 * hh + 1) * HEAD_DIM] = (
                    y[:, hh * HEAD_DIM:(hh + 1) * HEAD_DIM].astype(o_ref.dtype))
                o_ref[:, (2 * hh + 1) * HEAD_DIM:(2 * hh + 2) * HEAD_DIM] = jnp.ones(
                    (y.shape[0], HEAD_DIM), o_ref.dtype)
        elif kind == "silu":
            o_ref[...] = (y * _sigmoid(y)).astype(o_ref.dtype)
        elif kind == "sigmoid":
            o_ref[...] = _sigmoid(y).astype(o_ref.dtype)
        else:
            is_q = kind in ("rope_q", "normrope_q")
            norm = kind in ("normrope_q", "normrope_k")
            c, s = c_ref[...], s_ref[...]
            scale = _head_rms_scale(y) if norm else None
            for hh in range(width // HEAD_DIM):
                sl = slice(hh * HEAD_DIM, (hh + 1) * HEAD_DIM)
                yh = y[:, sl]
                if norm:
                    yh = yh * scale[:, sl] * (qg_ref[...] if is_q else kg_ref[...])
                yh = _rope(yh, c, s)
                if is_q:
                    yh = yh * Q_SCALE
                o_ref[:, sl] = yh.astype(o_ref.dtype)


def _inproj(X, mod, g, w, tabs, qg, kg, specs, dtypes, dims, tm, name):
    B, S, L = dims
    n = X.shape[0]
    ts = S // tm
    nlat = B * ts
    wtot = w.shape[1]

    def mod_idx(i):
        return (jnp.where(i < nlat, i // ts, B), 0, 0)

    def tab_idx(i):
        return (jnp.where(i < nlat, i % ts, ts), 0)

    widths = [2 * wd if kind == "value" else wd for kind, wd in specs]
    out_shape = [jax.ShapeDtypeStruct((n, wd), dt) for wd, dt in zip(widths, dtypes)]
    out_specs = [pl.BlockSpec((tm, wd), lambda i: (i, 0)) for wd in widths]
    tab_spec = pl.BlockSpec((tm, HEAD_DIM), tab_idx)
    return pl.pallas_call(
        functools.partial(_inproj_kernel, specs),
        grid=(n // tm,),
        in_specs=[
            pl.BlockSpec((tm, D_MODEL), lambda i: (i, 0)),
            pl.BlockSpec((None, 1, 3 * D_MODEL), mod_idx),
            pl.BlockSpec((1, D_MODEL), lambda i: (0, 0)),
            pl.BlockSpec((D_MODEL, wtot), lambda i: (0, 0)),
            tab_spec, tab_spec,
            pl.BlockSpec((1, HEAD_DIM), lambda i: (0, 0)),
            pl.BlockSpec((1, HEAD_DIM), lambda i: (0, 0)),
        ],
        out_specs=out_specs,
        out_shape=out_shape,
        compiler_params=_cparams(("parallel",)),
        name=name,
    )(X, mod, g, w, tabs[0], tabs[1], qg, kg)


def _lru_kernel(reverse, nchunk, u_ref, up_ref, un_ref, h0_ref, cw_ref, cb_ref, wg_ref, ba_ref, bx_ref, lam_ref,
                *rest):
    y_ref, hout_ref, ext_s, a_s, b_s, h_s = rest[-6:]
    K, T, _ = u_ref.shape
    c = pl.program_id(1)
    j = (nchunk - 1 - c) if reverse else c
    has_prev = j > 0
    has_next = j < nchunk - 1

    lam = lam_ref[...]
    neg = -lam
    softplus = jnp.maximum(neg, 0.0) + jnp.log1p(jnp.exp(-jnp.abs(neg)))
    c1 = (-0.5 * LRU_C * LOG2E) * softplus
    hba = 0.5 * ba_ref[...]
    hbx = 0.5 * bx_ref[...]
    cw = 0.5 * cw_ref[...]
    cb = 0.5 * cb_ref[...]

    for kb in range(K):
        ext_s[kb, 0:8, :] = jnp.where(has_prev, up_ref[kb], 0.0)
        ext_s[kb, 8:8 + T, :] = u_ref[kb]
        ext_s[kb, 8 + T:16 + T, :] = jnp.where(has_next, un_ref[kb], 0.0)
        ext = ext_s[kb]
        hu = cb
        for k in range(CONV_W):
            shift = (CONV_LEFT - k) % (T + 16)
            tap = ext if shift == 0 else pltpu.roll(ext, shift, 0)
            hu = hu + tap[8:8 + T, :] * cw[k:k + 1, :]
        ub = hu.astype(BF16)
        for q in range(D_MODEL // GATE_CHUNK):
            sl = slice(q * GATE_CHUNK, (q + 1) * GATE_CHUNK)
            z = jnp.dot(ub[:, sl], wg_ref[q], preferred_element_type=F32)
            tr = jnp.tanh(z[:, :GATE_CHUNK] + hba[:, sl])
            ti = jnp.tanh(z[:, GATE_CHUNK:] + hbx[:, sl])
            a = jnp.exp2(c1[:, sl] * tr + c1[:, sl])
            x = 1.0 - a * a
            root = jnp.where(x > 0.0, x * lax.rsqrt(x), 0.0)
            b = root * ((ti + 1.0) * hu[:, sl])
            for n in range(GATE_CHUNK // HEAD_DIM):
                lanes = slice(n * HEAD_DIM, (n + 1) * HEAD_DIM)
                sub = q * (GATE_CHUNK // HEAD_DIM) + n
                a_s[kb, pl.ds(sub, T, stride=8), :] = a[:, lanes]
                b_s[kb, pl.ds(sub, T, stride=8), :] = b[:, lanes]

    @pl.when(c == 0)
    def _():
        h_s[...] = h0_ref[...]

    def body(t, hs):
        tt = (T - 1 - t) if reverse else t
        rows = pl.ds(pl.multiple_of(tt * 8, 8), 8)
        out = []
        for kb in range(K):
            h = a_s[kb, rows, :] * hs[kb] + b_s[kb, rows, :]
            b_s[kb, rows, :] = h
            out.append(h)
        return tuple(out)

    hs = lax.fori_loop(0, T, body, tuple(h_s[kb] for kb in range(K)), unroll=4)
    for kb in range(K):
        h_s[kb] = hs[kb]
        for n in range(D_MODEL // HEAD_DIM):
            y_ref[kb, :, n * HEAD_DIM:(n + 1) * HEAD_DIM] = b_s[kb, pl.ds(n, T, stride=8), :].astype(y_ref.dtype)

    @pl.when(c == nchunk - 1)
    def _():
        hout_ref[...] = h_s[...]


def _lru_call(u3, h0, y_prev, weights, K, T, seg0, reverse, name):
    rows = u3.shape[1]
    nchunk = rows // T
    t8 = T // 8
    blk0 = seg0 // K

    def chunk(c):
        return (nchunk - 1 - c) if reverse else c

    main = pl.BlockSpec((K, T, D_MODEL), lambda g, c: (blk0 + g, chunk(c), 0))
    prev = pl.BlockSpec((K, 8, D_MODEL), lambda g, c: (blk0 + g, jnp.maximum(chunk(c) * t8 - 1, 0), 0))
    nxt = pl.BlockSpec((K, 8, D_MODEL),
                       lambda g, c: (blk0 + g, jnp.minimum((chunk(c) + 1) * t8, rows // 8 - 1), 0))
    hspec = pl.BlockSpec((None, K, 8, HEAD_DIM), lambda g, c: (g, 0, 0, 0))
    fixed = lambda shape: pl.BlockSpec(shape, lambda g, c: (0,) * len(shape))
    in_specs = [main, prev, nxt, hspec, fixed((CONV_W, D_MODEL)), fixed((1, D_MODEL)),
                fixed((D_MODEL // GATE_CHUNK, GATE_CHUNK, 2 * GATE_CHUNK)),
                fixed((1, D_MODEL)), fixed((1, D_MODEL)), fixed((1, D_MODEL))]
    args = [u3, u3, u3, h0, *weights]
    aliases = {}
    if y_prev is not None:
        in_specs.append(pl.BlockSpec(memory_space=pl.ANY))
        args.append(y_prev)
        aliases = {len(args) - 1: 0}
    return pl.pallas_call(
        functools.partial(_lru_kernel, reverse, nchunk),
        grid=(h0.shape[0], nchunk),
        in_specs=in_specs,
        out_specs=[main, hspec],
        out_shape=[jax.ShapeDtypeStruct(u3.shape, BF16), jax.ShapeDtypeStruct(h0.shape, F32)],
        input_output_aliases=aliases,
        scratch_shapes=[
            pltpu.VMEM((K, T + 16, D_MODEL), F32),
            pltpu.VMEM((K, T * 8, HEAD_DIM), F32),
            pltpu.VMEM((K, T * 8, HEAD_DIM), F32),
            pltpu.VMEM((K, 8, HEAD_DIM), F32),
        ],
        compiler_params=_cparams(("arbitrary", "arbitrary")),
        name=name,
    )(*args)


def _lru(uA, weights, dims, reverse, name):
    B, S, L = dims
    n = uA.shape[0]
    K = LRU_BATCH
    assert B % K == 0 and n % S == 0 and (B * S // L) % K == 0
    assert D_MODEL == 8 * HEAD_DIM
    h0 = jnp.zeros((B // K, K, 8, HEAD_DIM), F32)
    y, h = _lru_call(uA.reshape(n // L, L, D_MODEL), h0, None, weights, K, L, B * S // L, reverse, name + "_ctx")
    y, _ = _lru_call(uA.reshape(n // S, S, D_MODEL), h, y.reshape(n // S, S, D_MODEL), weights, K, L, 0, reverse,
                     name)
    return y.reshape(n, D_MODEL)


def _gate_weights(wa, wx):
    per = GATE_CHUNK // LRU_BLOCK_W
    nchunks = wa.shape[0] // per
    eye = jnp.eye(per, dtype=wa.dtype)

    def bd(w):
        w4 = w.reshape(nchunks, per, LRU_BLOCK_W, LRU_BLOCK_W)
        return jnp.einsum("jkde,kl->jkdle", w4, eye).reshape(nchunks, GATE_CHUNK, GATE_CHUNK)

    return jnp.concatenate([bd(wa), bd(wx)], axis=-1).astype(BF16)


def _row_shift(mrun):
    mb = mrun.astype(BF16)
    return jnp.broadcast_to(mb.max(axis=-1, keepdims=True), mb.shape).astype(F32)


def _shifted_exp2(s, shift):
    blocks = [jnp.exp2(s[:, c * HEAD_DIM:(c + 1) * HEAD_DIM] - shift) for c in range(s.shape[1] // HEAD_DIM)]
    return jnp.concatenate(blocks, axis=1).astype(BF16)


def _lane_fold(x, acc, op):
    for c in range(x.shape[1] // HEAD_DIM):
        blk = x[:, c * HEAD_DIM:(c + 1) * HEAD_DIM]
        acc = blk if acc is None else op(acc, blk)
    return acc


def _softmax_pv(scores, values, sink):
    mrun = None
    for s in scores:
        mrun = _lane_fold(s, mrun, jnp.maximum)
    m = mrun.max(axis=-1, keepdims=True)
    if sink is not None:
        m = jnp.maximum(m, sink)
    o = None
    for s, v in zip(scores, values):
        d = jnp.dot(jnp.exp2(s - m).astype(BF16), v, preferred_element_type=F32)
        o = d if o is None else o + d
    l = o[:, HEAD_DIM:]
    if sink is not None:
        l = l + jnp.exp2(sink - m)
    return o[:, :HEAD_DIM] * (1.0 / l)


def _qk(q, k):
    return lax.dot_general(q, k, (((1,), (1,)), ((), ())), preferred_element_type=F32)


def _stack_heads(q_ref, g):
    return jnp.concatenate(
        [q_ref[:, (g * GROUP + h) * HEAD_DIM:(g * GROUP + h + 1) * HEAD_DIM] for h in range(GROUP)], axis=0)


def _sink_col(sink_ref, g, tq):
    return jnp.concatenate(
        [jnp.full((tq, 1), sink_ref[g * GROUP + h] * LOG2E, F32) for h in range(GROUP)], axis=0)


def _sink_lanes(sink_ref, g, tq):
    return jnp.concatenate(
        [jnp.full((tq, HEAD_DIM), sink_ref[g * GROUP + h] * LOG2E, F32) for h in range(GROUP)], axis=0)


def _store_heads(o_ref, gate_ref, o, g, tq):
    for h in range(GROUP):
        sl = slice((g * GROUP + h) * HEAD_DIM, (g * GROUP + h + 1) * HEAD_DIM)
        o_ref[:, sl] = (o[h * tq:(h + 1) * tq] * gate_ref[:, sl].astype(F32)).astype(o_ref.dtype)


def _kv_slices(g):
    return slice(g * HEAD_DIM, (g + 1) * HEAD_DIM), slice(2 * g * HEAD_DIM, 2 * (g + 1) * HEAD_DIM)


def _window_kernel(nlat, S, sink_ref, q_ref, kl_ref, vl_ref, kc_ref, vc_ref, gate_ref, o_ref, s_a, s_b, m_a, m_b):
    tq = q_ref.shape[0]
    L = kc_ref.shape[0]
    span = tq + 2 * WINDOW
    i = pl.program_id(1)

    def band_start(tile):
        return pl.multiple_of(jnp.clip(tile * tq - WINDOW, 0, S - span), HEAD_DIM)

    def step(write, read):
        if write is not None:
            start = band_start(i)
            qpos = i * tq + lax.broadcasted_iota(jnp.int32, (tq, span), 0)
            kpos = start + lax.broadcasted_iota(jnp.int32, (tq, span), 1)
            bias = jnp.where(jnp.abs(kpos - qpos) <= WINDOW, 0.0, NEG_INF)
            bias = jnp.concatenate([bias] * GROUP, axis=0)
            for g in range(N_KV):
                cs, _ = _kv_slices(g)
                qs = _stack_heads(q_ref, g)
                s_ctx = _qk(qs, kc_ref[:, cs])
                s_lat = _qk(qs, kl_ref[pl.ds(start, span), cs]) + bias
                write[0][g, :, :L] = s_ctx
                write[0][g, :, L:] = s_lat
                mrun = _lane_fold(s_lat, _lane_fold(s_ctx, None, jnp.maximum), jnp.maximum)
                write[1][g] = jnp.maximum(_row_shift(mrun), _sink_lanes(sink_ref, g, tq))
        if read is not None:
            start = band_start(i - 1)
            for g in range(N_KV):
                _, vs = _kv_slices(g)
                m = read[1][g]
                o = jnp.dot(_shifted_exp2(read[0][g, :, :L], m), vc_ref[:, vs], preferred_element_type=F32)
                o = o + jnp.dot(_shifted_exp2(read[0][g, :, L:], m), vl_ref[pl.ds(start, span), vs],
                                preferred_element_type=F32)
                l = o[:, HEAD_DIM:] + jnp.exp2(_sink_lanes(sink_ref, g, tq) - m)
                _store_heads(o_ref, gate_ref, o[:, :HEAD_DIM] * (1.0 / l), g, tq)

    a, b = (s_a, m_a), (s_b, m_b)
    odd = i % 2 == 1
    steady = jnp.logical_and(i > 0, i < nlat)

    @pl.when(i == 0)
    def _():
        step(a, None)

    @pl.when(jnp.logical_and(steady, odd))
    def _():
        step(b, a)

    @pl.when(jnp.logical_and(steady, jnp.logical_not(odd)))
    def _():
        step(a, b)

    @pl.when(i == nlat)
    def _():
        step(None, b)


def _window_attention(q, k, v, gate, sink, dims, tq, rows):
    B, S, L = dims
    nlat = S // tq
    assert nlat % 2 == 0
    M = GROUP * tq
    keys = L + tq + 2 * WINDOW
    lat_spec = lambda wd: pl.BlockSpec((S, wd), lambda b, i: (b, 0))
    ctx_spec = lambda wd: pl.BlockSpec((L, wd), lambda b, i: (B * S // L + b, 0))
    return pl.pallas_call(
        functools.partial(_window_kernel, nlat, S),
        grid=(B, nlat + 1),
        in_specs=[pl.BlockSpec(memory_space=pltpu.SMEM),
                  pl.BlockSpec((tq, D_MODEL), lambda b, i: (b * nlat + jnp.minimum(i, nlat - 1), 0)),
                  lat_spec(KV_W), lat_spec(2 * KV_W), ctx_spec(KV_W), ctx_spec(2 * KV_W),
                  pl.BlockSpec((tq, D_MODEL), lambda b, i: (b * nlat + jnp.maximum(i - 1, 0), 0))],
        out_specs=pl.BlockSpec((tq, D_MODEL), lambda b, i: (b * nlat + jnp.maximum(i - 1, 0), 0)),
        out_shape=jax.ShapeDtypeStruct((rows, D_MODEL), BF16),
        scratch_shapes=[pltpu.VMEM((N_KV, M, keys), F32)] * 2 + [pltpu.VMEM((N_KV, M, HEAD_DIM), F32)] * 2,
        compiler_params=_cparams(("arbitrary", "arbitrary")),
        name="attn_window",
    )(sink, q, k, v, k, v, gate)


def _dense_kernel(nlat, q_ref, kl_ref, vl_ref, kc_ref, vc_ref, gate_ref, o_ref, s_a, s_b, m_a, m_b):
    tq = q_ref.shape[0]
    L = kc_ref.shape[0]
    S = kl_ref.shape[0]
    i = pl.program_id(1)
    bounds = [(0, L)] + [(L + j0, L + j0 + KEY_CHUNK) for j0 in range(0, S, KEY_CHUNK)]

    def step(write, read):
        for g in range(N_KV):
            if write is None:
                break
            cs, _ = _kv_slices(g)
            qs = _stack_heads(q_ref, g)
            mrun = None
            for lo, hi in bounds:
                k = kc_ref[:, cs] if lo == 0 else kl_ref[lo - L:hi - L, cs]
                s = _qk(qs, k)
                write[0][g, :, lo:hi] = s
                mrun = _lane_fold(s, mrun, jnp.maximum)
            write[1][g] = _row_shift(mrun)
        for g in range(N_KV):
            if read is None:
                break
            _, vs = _kv_slices(g)
            m = read[1][g]
            o = None
            for lo, hi in bounds:
                v = vc_ref[:, vs] if lo == 0 else vl_ref[lo - L:hi - L, vs]
                d = jnp.dot(_shifted_exp2(read[0][g, :, lo:hi], m), v, preferred_element_type=F32)
                o = d if o is None else o + d
            _store_heads(o_ref, gate_ref, o[:, :HEAD_DIM] * (1.0 / o[:, HEAD_DIM:]), g, tq)

    a, b = (s_a, m_a), (s_b, m_b)
    odd = i % 2 == 1
    steady = jnp.logical_and(i > 0, i < nlat)

    @pl.when(i == 0)
    def _():
        step(a, None)

    @pl.when(jnp.logical_and(steady, odd))
    def _():
        step(b, a)

    @pl.when(jnp.logical_and(steady, jnp.logical_not(odd)))
    def _():
        step(a, b)

    @pl.when(i == nlat)
    def _():
        step(None, b)


def _dense_attention(q, k, v, gate, dims, tq, rows):
    B, S, L = dims
    nlat = S // tq
    assert nlat % 2 == 0
    one = pl.Buffered(1)
    return pl.pallas_call(
        functools.partial(_dense_kernel, nlat),
        grid=(B, nlat + 1),
        in_specs=[
            pl.BlockSpec((tq, D_MODEL), lambda b, i: (b * nlat + jnp.minimum(i, nlat - 1), 0)),
            pl.BlockSpec((S, KV_W), lambda b, i: (b, 0)),
            pl.BlockSpec((S, 2 * KV_W), lambda b, i: (b, 0)),
            pl.BlockSpec((L, KV_W), lambda b, i: (B * S // L + b, 0), pipeline_mode=one),
            pl.BlockSpec((L, 2 * KV_W), lambda b, i: (B * S // L + b, 0), pipeline_mode=one),
            pl.BlockSpec((tq, D_MODEL), lambda b, i: (b * nlat + jnp.maximum(i - 1, 0), 0)),
        ],
        out_specs=pl.BlockSpec((tq, D_MODEL), lambda b, i: (b * nlat + jnp.maximum(i - 1, 0), 0)),
        out_shape=jax.ShapeDtypeStruct((rows, D_MODEL), BF16),
        scratch_shapes=[pltpu.VMEM((N_KV, GROUP * tq, L + S), F32)] * 2
        + [pltpu.VMEM((N_KV, GROUP * tq, HEAD_DIM), F32)] * 2,
        compiler_params=_cparams(("arbitrary", "arbitrary")),
        name="attn_dense",
    )(q, k, v, k, v, gate)


def _ctx_kernel(sink_ref, qb_ref, kb_ref, vb_ref, gb_ref, qc_ref, kc_ref, vc_ref, gc_ref, yb_any, yc_any,
                ob_ref, oc_ref):
    del yb_any, yc_any
    tq = qb_ref.shape[0]
    for g in range(N_KV):
        cs, vs = _kv_slices(g)
        qs = _stack_heads(qb_ref, g)
        ob = _softmax_pv([_qk(qs, kb_ref[:, cs])], [vb_ref[:, vs]], _sink_col(sink_ref, g, tq))
        _store_heads(ob_ref, gb_ref, ob, g, tq)
        qs = _stack_heads(qc_ref, g)
        oc = _softmax_pv([_qk(qs, kc_ref[:, cs])], [vc_ref[:, vs]], None)
        _store_heads(oc_ref, gc_ref, oc, g, tq)


def _ctx_attention(sink, qB, kB, vB, gB, qC, kC, vC, gC, yB, yC, dims):
    B, S, L = dims
    c0 = B * S // L
    rows = lambda wd: pl.BlockSpec((L, wd), lambda b: (c0 + b, 0))
    anyspec = pl.BlockSpec(memory_space=pl.ANY)
    return pl.pallas_call(
        _ctx_kernel,
        grid=(B,),
        in_specs=[pl.BlockSpec(memory_space=pltpu.SMEM),
                  rows(D_MODEL), rows(KV_W), rows(2 * KV_W), rows(D_MODEL),
                  rows(D_MODEL), rows(KV_W), rows(2 * KV_W), rows(D_MODEL),
                  anyspec, anyspec],
        out_specs=[rows(D_MODEL), rows(D_MODEL)],
        out_shape=[jax.ShapeDtypeStruct(yB.shape, BF16), jax.ShapeDtypeStruct(yC.shape, BF16)],
        input_output_aliases={9: 0, 10: 1},
        compiler_params=_cparams(("arbitrary",)),
        name="attn_ctx",
    )(sink, qB, kB, vB, gB, qC, kC, vC, gC, yB, yC)


def _merge_kernel(final, x_ref, mod_ref, yf_ref, yr_ref, ga_ref, yb_ref, yc_ref, m_ref,
                  wb_ref, wo_ref, fg_ref, o_ref):
    ya = (yf_ref[...].astype(F32) + yr_ref[...].astype(F32)) * ga_ref[...].astype(F32)
    ts = (ya.astype(BF16), yb_ref[...], yc_ref[...])
    acc = 0.0
    for n in range(3):
        br = jnp.dot(ts[n], wb_ref[n], preferred_element_type=F32)
        acc = acc + m_ref[:, n * D_MODEL:(n + 1) * D_MODEL].astype(F32) * br
    y = jnp.dot(acc.astype(BF16), wo_ref[...], preferred_element_type=F32)
    xn = x_ref[...] + mod_ref[:, 2 * D_MODEL:] * y
    if final:
        xn = xn * lax.rsqrt(jnp.mean(xn * xn, axis=-1, keepdims=True) + EPS) * fg_ref[...]
    o_ref[...] = xn


def _merge(X, mod, yf, yr, ga, yb, yc, m, wb, wo, fg, dims, tm, final):
    B, S, L = dims
    n = B * S if final else X.shape[0]
    ts = S // tm
    nlat = B * ts

    def mod_idx(i):
        return (jnp.where(i < nlat, i // ts, B), 0, 0)

    rows = lambda wd: pl.BlockSpec((tm, wd), lambda i: (i, 0))
    return pl.pallas_call(
        functools.partial(_merge_kernel, final),
        grid=(n // tm,),
        in_specs=[
            rows(D_MODEL),
            pl.BlockSpec((None, 1, 3 * D_MODEL), mod_idx),
            rows(D_MODEL), rows(D_MODEL), rows(D_MODEL), rows(D_MODEL), rows(D_MODEL), rows(3 * D_MODEL),
            pl.BlockSpec((3, D_MODEL, D_MODEL), lambda i: (0, 0, 0), pipeline_mode=pl.Buffered(1)),
            pl.BlockSpec((D_MODEL, D_MODEL), lambda i: (0, 0), pipeline_mode=pl.Buffered(1)),
            pl.BlockSpec((1, D_MODEL), lambda i: (0, 0)),
        ],
        out_specs=rows(D_MODEL),
        out_shape=jax.ShapeDtypeStruct((n, D_MODEL), F32),
        compiler_params=_cparams(("parallel",)),
        name="merge_final" if final else "merge",
    )(X, mod, yf, yr, ga, yb, yc, m, wb, wo, fg)


def _rope_tables(S, tm):
    t = jnp.arange(S)
    P = HEAD_DIM // 4
    inv = ROPE_THETA ** (-jnp.arange(P, dtype=F32) / P)
    ang = jnp.stack([(t // GRID_W)[:, None] * inv, (t % GRID_W)[:, None] * inv], axis=1)
    cos = jnp.cos(ang).reshape(S, HEAD_DIM // 2)
    sin = jnp.sin(ang).reshape(S, HEAD_DIM // 2)
    c = jnp.concatenate([cos, cos], axis=1)
    s = jnp.concatenate([-sin, sin], axis=1)
    ident = jnp.zeros((tm, HEAD_DIM), F32)
    return jnp.concatenate([c, ident + 1.0]), jnp.concatenate([s, ident])


def kernel(x, c, ctx, c_ctx, norm_g, w_mod, b_mod, w_in, conv_w, conv_b, lru_wa, lru_ba, lru_wx, lru_bx,
           lru_lambda, attn_sink, q_norm_g, k_norm_g, w_branch, w_out, final_g):
    B, S, _ = x.shape
    L = ctx.shape[1]
    depth = w_in.shape[0]
    dims = (B, S, L)
    assert S % L == 0 and L % 128 == 0 and S % GRID_W == 0 and S >= 256 + 2 * WINDOW
    tm = math.gcd(1024, math.gcd(S, B * L))
    tm_merge = math.gcd(512, tm)
    tq_w = 256
    tq_g = 128

    X = jnp.concatenate([x.reshape(B * S, D_MODEL), ctx.reshape(B * L, D_MODEL)], axis=0)
    rows = -(-(B + 1) // 8) * 8
    cc = jnp.concatenate([c, c_ctx[None], jnp.zeros((rows - B - 1, D_MODEL), F32)], axis=0)
    mod_all = _modulation(cc, w_mod, b_mod).reshape(depth, rows, 1, 3 * D_MODEL)
    tabs = _rope_tables(S, tm)
    w_in16 = w_in.astype(BF16)
    cuts = (2 * D_MODEL, 3 * D_MODEL, 3 * D_MODEL + KV_W, 4 * D_MODEL + 2 * KV_W, 5 * D_MODEL + 2 * KV_W,
            5 * D_MODEL + 3 * KV_W)
    w_in16 = jnp.concatenate(
        [w_in16[..., :cuts[0]], _pair_layout(w_in16[..., cuts[0]:cuts[1]]), _pair_layout(w_in16[..., cuts[1]:cuts[2]]),
         w_in16[..., cuts[2]:cuts[3]], _pair_layout(w_in16[..., cuts[3]:cuts[4]]),
         _pair_layout(w_in16[..., cuts[4]:cuts[5]]), w_in16[..., cuts[5]:]], axis=-1)
    wb16 = w_branch.astype(BF16)
    wo16 = w_out.astype(BF16)
    row = lambda a: a.reshape(1, -1)

    a_specs = (("plain", D_MODEL), ("silu", D_MODEL))
    b_specs = (("rope_q", D_MODEL), ("rope_k", KV_W), ("value", KV_W), ("silu", D_MODEL))
    c_specs = (("normrope_q", D_MODEL), ("normrope_k", KV_W), ("value", KV_W), ("silu", D_MODEL))
    m_specs = (("sigmoid", 3 * D_MODEL),)
    a_w = 2 * D_MODEL
    b_w = 2 * D_MODEL + 2 * KV_W

    out = None
    for l in range(depth):
        last = l == depth - 1
        mod = mod_all[l]
        g = row(norm_g[l])
        qg, kg = _pair_layout(row(q_norm_g[l])), _pair_layout(row(k_norm_g[l]))
        w = w_in16[l]
        proj = functools.partial(_inproj, X, mod, g, tabs=tabs, qg=qg, kg=kg, dims=dims, tm=tm)
        uA, gA = proj(w=w[:, :a_w], specs=a_specs, dtypes=(F32, BF16), name="inproj_a")
        qB, kB, vB, gB = proj(w=w[:, a_w:a_w + b_w], specs=b_specs, dtypes=(BF16,) * 4, name="inproj_b")
        qC, kC, vC, gC = proj(w=w[:, a_w + b_w:a_w + 2 * b_w], specs=c_specs, dtypes=(BF16,) * 4,
                              name="inproj_c")
        (m,) = proj(w=w[:, a_w + 2 * b_w:], specs=m_specs, dtypes=(BF16,), name="inproj_m")

        ys = []
        for d, rev in enumerate((False, True)):
            wg = _gate_weights(lru_wa[l, d], lru_wx[l, d])
            weights = (conv_w[l], row(conv_b[l]), wg, row(lru_ba[l, d]), row(lru_bx[l, d]), row(lru_lambda[l, d]))
            ys.append(_lru(uA, weights, dims, rev, "lru_rev" if rev else "lru_fwd"))
        rows_out = B * S if last else X.shape[0]
        yB = _window_attention(qB, kB, vB, gB, attn_sink[l], dims, tq_w, rows_out)
        yC = _dense_attention(qC, kC, vC, gC, dims, tq_g, rows_out)
        if not last:
            yB, yC = _ctx_attention(attn_sink[l], qB, kB, vB, gB, qC, kC, vC, gC, yB, yC, dims)
        res = _merge(X, mod, ys[0], ys[1], gA, yB, yC, m, wb16[l], wo16[l], row(final_g), dims, tm_merge, last)
        if last:
            out = res.reshape(B, S, D_MODEL)
        else:
            X = res
    return out
```

```python
import functools
import math

import jax
import jax.numpy as jnp
from jax import lax
from jax.experimental import pallas as pl
from jax.experimental.pallas import tpu as pltpu

D_MODEL = 1024
HEAD_DIM = 128
N_HEADS = D_MODEL // HEAD_DIM
N_KV = N_HEADS // 4
GROUP = N_HEADS // N_KV
KV_W = N_KV * HEAD_DIM
LRU_BLOCK_W = 64
LRU_C = 8.0
CONV_W = 4
CONV_LEFT = 2
WINDOW = 128
GRID_W = 64
ROPE_THETA = 10000.0
EPS = 1e-6
NEG_INF = -1e30
LOG2E = 1.4426950408889634
Q_SCALE = HEAD_DIM ** -0.5 * LOG2E
GATE_CHUNK = 256
KEY_CHUNK = 512
LRU_BATCH = 4
VMEM_LIMIT = 56 * 1024 * 1024

F32 = jnp.float32
BF16 = jnp.bfloat16


def _cparams(sem):
    return pltpu.CompilerParams(dimension_semantics=sem, vmem_limit_bytes=VMEM_LIMIT)


def _sigmoid(x):
    return 0.5 * (jnp.tanh(0.5 * x) + 1.0)


def _mod_kernel(cc_ref, w_ref, b_ref, o_ref):
    cc = cc_ref[...]
    s = cc * _sigmoid(cc)
    o_ref[...] = jnp.dot(s, w_ref[...], preferred_element_type=F32,
                         precision=lax.Precision.HIGHEST) + b_ref[...]


def _modulation(cc, w_mod, b_mod):
    depth = w_mod.shape[0]
    rows = cc.shape[0]
    return pl.pallas_call(
        _mod_kernel,
        grid=(depth, 3),
        in_specs=[
            pl.BlockSpec((rows, D_MODEL), lambda l, j: (0, 0)),
            pl.BlockSpec((None, D_MODEL, D_MODEL), lambda l, j: (l, 0, j)),
            pl.BlockSpec((None, 1, D_MODEL), lambda l, j: (l, 0, j)),
        ],
        out_specs=pl.BlockSpec((None, rows, D_MODEL), lambda l, j: (l, 0, j)),
        out_shape=jax.ShapeDtypeStruct((depth, rows, 3 * D_MODEL), F32),
        compiler_params=_cparams(("arbitrary", "arbitrary")),
        name="modulation",
    )(cc, w_mod, b_mod.reshape(depth, 1, 3 * D_MODEL))


def _norm_mod(x, g, mod):
    ms = jnp.mean(x * x, axis=-1, keepdims=True)
    y = x * lax.rsqrt(ms + EPS) * g
    return y * (1.0 + mod[:, D_MODEL:2 * D_MODEL]) + mod[:, :D_MODEL]


def _pair_layout(w):
    lead = w.shape[:-1]
    nh = w.shape[-1] // HEAD_DIM
    return w.reshape(*lead, nh, 2, 2, HEAD_DIM // 4).swapaxes(-3, -2).reshape(*lead, nh * HEAD_DIM)


def _rope(yh, c, s):
    return yh * c + pltpu.roll(yh, HEAD_DIM // 2, 1) * s


def _head_rms_scale(y):
    two = 2 * HEAD_DIM
    rows = lax.broadcasted_iota(jnp.int32, (two, two), 0) // HEAD_DIM
    cols = lax.broadcasted_iota(jnp.int32, (two, two), 1) // HEAD_DIM
    avg = jnp.where(rows == cols, 1.0 / HEAD_DIM, 0.0).astype(BF16)
    sq = (y * y).astype(BF16)
    ms = [jnp.dot(sq[:, p * two:(p + 1) * two], avg, preferred_element_type=F32) for p in range(y.shape[1] // two)]
    return lax.rsqrt(jnp.concatenate(ms, axis=1) + EPS)


def _row_source_specs(X, Xc, tm, nlat):
    if Xc is None:
        return [pl.BlockSpec((tm, D_MODEL), lambda i: (i, 0)), pl.BlockSpec((tm, D_MODEL), lambda i: (0, 0))], (X, X)
    return [pl.BlockSpec((tm, D_MODEL), lambda i: (jnp.minimum(i, nlat - 1), 0)),
            pl.BlockSpec((tm, D_MODEL), lambda i: (jnp.maximum(i - nlat, 0), 0))], (X, Xc)


def _rows(split, x_ref, xc_ref):
    if split is None:
        return x_ref[...]
    return jnp.where(pl.program_id(0) < split, x_ref[...], xc_ref[...])


def _inproj_kernel(specs, split, x_ref, xc_ref, mod_ref, g_ref, w_ref, c_ref, s_ref, qg_ref, kg_ref, *out_refs):
    h = _norm_mod(_rows(split, x_ref, xc_ref), g_ref[...], mod_ref[...]).astype(BF16)
    off = 0
    for (kind, width), o_ref in zip(specs, out_refs):
        y = jnp.dot(h, w_ref[:, off:off + width], preferred_element_type=F32)
        off += width
        if kind == "plain":
            o_ref[...] = y.astype(o_ref.dtype)
        elif kind == "value":
            for hh in range(width // HEAD_DIM):
                o_ref[:, 2 * hh * HEAD_DIM:(2 * hh + 1) * HEAD_DIM] = (
                    y[:, hh * HEAD_DIM:(hh + 1) * HEAD_DIM].astype(o_ref.dtype))
                o_ref[:, (2 * hh + 1) * HEAD_DIM:(2 * hh + 2) * HEAD_DIM] = jnp.ones(
                    (y.shape[0], HEAD_DIM), o_ref.dtype)
        elif kind == "silu":
            o_ref[...] = (y * _sigmoid(y)).astype(o_ref.dtype)
        elif kind == "sigmoid":
            o_ref[...] = _sigmoid(y).astype(o_ref.dtype)
        else:
            is_q = kind in ("rope_q", "normrope_q")
            norm = kind in ("normrope_q", "normrope_k")
            c, s = c_ref[...], s_ref[...]
            scale = _head_rms_scale(y) if norm else None
            for hh in range(width // HEAD_DIM):
                sl = slice(hh * HEAD_DIM, (hh + 1) * HEAD_DIM)
                yh = y[:, sl]
                if norm:
                    yh = yh * scale[:, sl] * (qg_ref[...] if is_q else kg_ref[...])
                yh = _rope(yh, c, s)
                if is_q:
                    yh = yh * Q_SCALE
                o_ref[:, sl] = yh.astype(o_ref.dtype)


def _inproj(X, Xc, mod, g, w, tabs, qg, kg, specs, dtypes, dims, tm, name):
    B, S, L = dims
    n = B * (S + L)
    ts = S // tm
    nlat = B * ts
    wtot = w.shape[1]

    def mod_idx(i):
        return (jnp.where(i < nlat, i // ts, B), 0, 0)

    def tab_idx(i):
        return (jnp.where(i < nlat, i % ts, ts), 0)

    widths = [2 * wd if kind == "value" else wd for kind, wd in specs]
    out_shape = [jax.ShapeDtypeStruct((n, wd), dt) for wd, dt in zip(widths, dtypes)]
    out_specs = [pl.BlockSpec((tm, wd), lambda i: (i, 0)) for wd in widths]
    tab_spec = pl.BlockSpec((tm, HEAD_DIM), tab_idx)
    x_specs, x_args = _row_source_specs(X, Xc, tm, nlat)
    return pl.pallas_call(
        functools.partial(_inproj_kernel, specs, None if Xc is None else nlat),
        grid=(n // tm,),
        in_specs=x_specs + [
            pl.BlockSpec((None, 1, 3 * D_MODEL), mod_idx),
            pl.BlockSpec((1, D_MODEL), lambda i: (0, 0)),
            pl.BlockSpec((D_MODEL, wtot), lambda i: (0, 0)),
            tab_spec, tab_spec,
            pl.BlockSpec((1, HEAD_DIM), lambda i: (0, 0)),
            pl.BlockSpec((1, HEAD_DIM), lambda i: (0, 0)),
        ],
        out_specs=out_specs,
        out_shape=out_shape,
        compiler_params=_cparams(("parallel",)),
        name=name,
    )(*x_args, mod, g, w, tabs[0], tabs[1], qg, kg)


def _lru_kernel(reverse, nchunk, u_ref, up_ref, un_ref, h0_ref, cw_ref, cb_ref, wg_ref, ba_ref, bx_ref, lam_ref,
                *rest):
    y_ref, hout_ref, ext_s, a_s, b_s, h_s = rest[-6:]
    K, T, _ = u_ref.shape
    c = pl.program_id(1)
    j = (nchunk - 1 - c) if reverse else c
    has_prev = j > 0
    has_next = j < nchunk - 1

    lam = lam_ref[...]
    neg = -lam
    softplus = jnp.maximum(neg, 0.0) + jnp.log1p(jnp.exp(-jnp.abs(neg)))
    c1 = (-0.5 * LRU_C * LOG2E) * softplus
    hba = 0.5 * ba_ref[...]
    hbx = 0.5 * bx_ref[...]
    cw = 0.5 * cw_ref[...]
    cb = 0.5 * cb_ref[...]

    for kb in range(K):
        ext_s[kb, 0:8, :] = jnp.where(has_prev, up_ref[kb], 0.0)
        ext_s[kb, 8:8 + T, :] = u_ref[kb]
        ext_s[kb, 8 + T:16 + T, :] = jnp.where(has_next, un_ref[kb], 0.0)
        ext = ext_s[kb]
        hu = cb
        for k in range(CONV_W):
            shift = (CONV_LEFT - k) % (T + 16)
            tap = ext if shift == 0 else pltpu.roll(ext, shift, 0)
            hu = hu + tap[8:8 + T, :] * cw[k:k + 1, :]
        ub = hu.astype(BF16)
        for q in range(D_MODEL // GATE_CHUNK):
            sl = slice(q * GATE_CHUNK, (q + 1) * GATE_CHUNK)
            z = jnp.dot(ub[:, sl], wg_ref[q], preferred_element_type=F32)
            tr = jnp.tanh(z[:, :GATE_CHUNK] + hba[:, sl])
            ti = jnp.tanh(z[:, GATE_CHUNK:] + hbx[:, sl])
            a = jnp.exp2(c1[:, sl] * tr + c1[:, sl])
            x = 1.0 - a * a
            root = jnp.where(x > 0.0, x * lax.rsqrt(x), 0.0)
            b = root * ((ti + 1.0) * hu[:, sl])
            for n in range(GATE_CHUNK // HEAD_DIM):
                lanes = slice(n * HEAD_DIM, (n + 1) * HEAD_DIM)
                sub = q * (GATE_CHUNK // HEAD_DIM) + n
                a_s[kb, pl.ds(sub, T, stride=8), :] = a[:, lanes]
                b_s[kb, pl.ds(sub, T, stride=8), :] = b[:, lanes]

    @pl.when(c == 0)
    def _():
        h_s[...] = h0_ref[...]

    def body(t, hs):
        tt = (T - 1 - t) if reverse else t
        rows = pl.ds(pl.multiple_of(tt * 8, 8), 8)
        out = []
        for kb in range(K):
            h = a_s[kb, rows, :] * hs[kb] + b_s[kb, rows, :]
            b_s[kb, rows, :] = h
            out.append(h)
        return tuple(out)

    hs = lax.fori_loop(0, T, body, tuple(h_s[kb] for kb in range(K)), unroll=4)
    for kb in range(K):
        h_s[kb] = hs[kb]
        for n in range(D_MODEL // HEAD_DIM):
            y_ref[kb, :, n * HEAD_DIM:(n + 1) * HEAD_DIM] = b_s[kb, pl.ds(n, T, stride=8), :].astype(y_ref.dtype)

    @pl.when(c == nchunk - 1)
    def _():
        hout_ref[...] = h_s[...]


def _lru_call(u3, h0, y_prev, weights, K, T, seg0, reverse, name):
    rows = u3.shape[1]
    nchunk = rows // T
    t8 = T // 8
    blk0 = seg0 // K

    def chunk(c):
        return (nchunk - 1 - c) if reverse else c

    main = pl.BlockSpec((K, T, D_MODEL), lambda g, c: (blk0 + g, chunk(c), 0))
    prev = pl.BlockSpec((K, 8, D_MODEL), lambda g, c: (blk0 + g, jnp.maximum(chunk(c) * t8 - 1, 0), 0))
    nxt = pl.BlockSpec((K, 8, D_MODEL),
                       lambda g, c: (blk0 + g, jnp.minimum((chunk(c) + 1) * t8, rows // 8 - 1), 0))
    hspec = pl.BlockSpec((None, K, 8, HEAD_DIM), lambda g, c: (g, 0, 0, 0))
    fixed = lambda shape: pl.BlockSpec(shape, lambda g, c: (0,) * len(shape))
    in_specs = [main, prev, nxt, hspec, fixed((CONV_W, D_MODEL)), fixed((1, D_MODEL)),
                fixed((D_MODEL // GATE_CHUNK, GATE_CHUNK, 2 * GATE_CHUNK)),
                fixed((1, D_MODEL)), fixed((1, D_MODEL)), fixed((1, D_MODEL))]
    args = [u3, u3, u3, h0, *weights]
    aliases = {}
    if y_prev is not None:
        in_specs.append(pl.BlockSpec(memory_space=pl.ANY))
        args.append(y_prev)
        aliases = {len(args) - 1: 0}
    return pl.pallas_call(
        functools.partial(_lru_kernel, reverse, nchunk),
        grid=(h0.shape[0], nchunk),
        in_specs=in_specs,
        out_specs=[main, hspec],
        out_shape=[jax.ShapeDtypeStruct(u3.shape, BF16), jax.ShapeDtypeStruct(h0.shape, F32)],
        input_output_aliases=aliases,
        scratch_shapes=[
            pltpu.VMEM((K, T + 16, D_MODEL), F32),
            pltpu.VMEM((K, T * 8, HEAD_DIM), F32),
            pltpu.VMEM((K, T * 8, HEAD_DIM), F32),
            pltpu.VMEM((K, 8, HEAD_DIM), F32),
        ],
        compiler_params=_cparams(("arbitrary", "arbitrary")),
        name=name,
    )(*args)


def _lru(uA, weights, dims, reverse, name):
    B, S, L = dims
    n = uA.shape[0]
    K = LRU_BATCH
    assert B % K == 0 and n % S == 0 and (B * S // L) % K == 0
    assert D_MODEL == 8 * HEAD_DIM
    h0 = jnp.zeros((B // K, K, 8, HEAD_DIM), F32)
    y, h = _lru_call(uA.reshape(n // L, L, D_MODEL), h0, None, weights, K, L, B * S // L, reverse, name + "_ctx")
    y, _ = _lru_call(uA.reshape(n // S, S, D_MODEL), h, y.reshape(n // S, S, D_MODEL), weights, K, L, 0, reverse,
                     name)
    return y.reshape(n, D_MODEL)


def _gate_weights(wa, wx):
    per = GATE_CHUNK // LRU_BLOCK_W
    nchunks = wa.shape[0] // per
    eye = jnp.eye(per, dtype=wa.dtype)

    def bd(w):
        w4 = w.reshape(nchunks, per, LRU_BLOCK_W, LRU_BLOCK_W)
        return jnp.einsum("jkde,kl->jkdle", w4, eye).reshape(nchunks, GATE_CHUNK, GATE_CHUNK)

    return jnp.concatenate([bd(wa), bd(wx)], axis=-1).astype(BF16)


def _row_shift(mrun):
    mb = mrun.astype(BF16)
    return jnp.broadcast_to(mb.max(axis=-1, keepdims=True), mb.shape).astype(F32)


def _shifted_exp2(s, shift):
    blocks = [jnp.exp2(s[:, c * HEAD_DIM:(c + 1) * HEAD_DIM] - shift) for c in range(s.shape[1] // HEAD_DIM)]
    return jnp.concatenate(blocks, axis=1).astype(BF16)


def _lane_fold(x, acc, op):
    for c in range(x.shape[1] // HEAD_DIM):
        blk = x[:, c * HEAD_DIM:(c + 1) * HEAD_DIM]
        acc = blk if acc is None else op(acc, blk)
    return acc


def _softmax_pv(scores, values, sink):
    mrun = None
    for s in scores:
        mrun = _lane_fold(s, mrun, jnp.maximum)
    m = mrun.max(axis=-1, keepdims=True)
    if sink is not None:
        m = jnp.maximum(m, sink)
    o = None
    for s, v in zip(scores, values):
        d = jnp.dot(jnp.exp2(s - m).astype(BF16), v, preferred_element_type=F32)
        o = d if o is None else o + d
    l = o[:, HEAD_DIM:]
    if sink is not None:
        l = l + jnp.exp2(sink - m)
    return o[:, :HEAD_DIM] * (1.0 / l)


def _qk(q, k):
    return lax.dot_general(q, k, (((1,), (1,)), ((), ())), preferred_element_type=F32)


def _stack_heads(q_ref, g):
    return jnp.concatenate(
        [q_ref[:, (g * GROUP + h) * HEAD_DIM:(g * GROUP + h + 1) * HEAD_DIM] for h in range(GROUP)], axis=0)


def _sink_col(sink_ref, g, tq):
    return jnp.concatenate(
        [jnp.full((tq, 1), sink_ref[g * GROUP + h] * LOG2E, F32) for h in range(GROUP)], axis=0)


def _sink_lanes(sink_ref, g, tq):
    return jnp.concatenate(
        [jnp.full((tq, HEAD_DIM), sink_ref[g * GROUP + h] * LOG2E, F32) for h in range(GROUP)], axis=0)


def _store_heads(o_ref, gate_ref, o, g, tq):
    for h in range(GROUP):
        sl = slice((g * GROUP + h) * HEAD_DIM, (g * GROUP + h + 1) * HEAD_DIM)
        o_ref[:, sl] = (o[h * tq:(h + 1) * tq] * gate_ref[:, sl].astype(F32)).astype(o_ref.dtype)


def _kv_slices(g):
    return slice(g * HEAD_DIM, (g + 1) * HEAD_DIM), slice(2 * g * HEAD_DIM, 2 * (g + 1) * HEAD_DIM)


def _window_kernel(nlat, S, sink_ref, q_ref, kl_ref, vl_ref, kc_ref, vc_ref, gate_ref, o_ref, s_a, s_b, m_a, m_b):
    tq = q_ref.shape[0]
    L = kc_ref.shape[0]
    span = tq + 2 * WINDOW
    i = pl.program_id(1)

    def band_start(tile):
        return pl.multiple_of(jnp.clip(tile * tq - WINDOW, 0, S - span), HEAD_DIM)

    def step(write, read):
        if write is not None:
            start = band_start(i)
            qpos = i * tq + lax.broadcasted_iota(jnp.int32, (tq, span), 0)
            kpos = start + lax.broadcasted_iota(jnp.int32, (tq, span), 1)
            bias = jnp.where(jnp.abs(kpos - qpos) <= WINDOW, 0.0, NEG_INF)
            bias = jnp.concatenate([bias] * GROUP, axis=0)
            for g in range(N_KV):
                cs, _ = _kv_slices(g)
                qs = _stack_heads(q_ref, g)
                s_ctx = _qk(qs, kc_ref[:, cs])
                s_lat = _qk(qs, kl_ref[pl.ds(start, span), cs]) + bias
                write[0][g, :, :L] = s_ctx
                write[0][g, :, L:] = s_lat
                mrun = _lane_fold(s_lat, _lane_fold(s_ctx, None, jnp.maximum), jnp.maximum)
                write[1][g] = jnp.maximum(_row_shift(mrun), _sink_lanes(sink_ref, g, tq))
        if read is not None:
            start = band_start(i - 1)
            for g in range(N_KV):
                _, vs = _kv_slices(g)
                m = read[1][g]
                o = jnp.dot(_shifted_exp2(read[0][g, :, :L], m), vc_ref[:, vs], preferred_element_type=F32)
                o = o + jnp.dot(_shifted_exp2(read[0][g, :, L:], m), vl_ref[pl.ds(start, span), vs],
                                preferred_element_type=F32)
                l = o[:, HEAD_DIM:] + jnp.exp2(_sink_lanes(sink_ref, g, tq) - m)
                _store_heads(o_ref, gate_ref, o[:, :HEAD_DIM] * (1.0 / l), g, tq)

    a, b = (s_a, m_a), (s_b, m_b)
    odd = i % 2 == 1
    steady = jnp.logical_and(i > 0, i < nlat)

    @pl.when(i == 0)
    def _():
        step(a, None)

    @pl.when(jnp.logical_and(steady, odd))
    def _():
        step(b, a)

    @pl.when(jnp.logical_and(steady, jnp.logical_not(odd)))
    def _():
        step(a, b)

    @pl.when(i == nlat)
    def _():
        step(None, b)


def _window_attention(q, k, v, gate, sink, dims, tq, rows):
    B, S, L = dims
    nlat = S // tq
    assert nlat % 2 == 0
    M = GROUP * tq
    keys = L + tq + 2 * WINDOW
    lat_spec = lambda wd: pl.BlockSpec((S, wd), lambda b, i: (b, 0))
    ctx_spec = lambda wd: pl.BlockSpec((L, wd), lambda b, i: (B * S // L + b, 0))
    return pl.pallas_call(
        functools.partial(_window_kernel, nlat, S),
        grid=(B, nlat + 1),
        in_specs=[pl.BlockSpec(memory_space=pltpu.SMEM),
                  pl.BlockSpec((tq, D_MODEL), lambda b, i: (b * nlat + jnp.minimum(i, nlat - 1), 0)),
                  lat_spec(KV_W), lat_spec(2 * KV_W), ctx_spec(KV_W), ctx_spec(2 * KV_W),
                  pl.BlockSpec((tq, D_MODEL), lambda b, i: (b * nlat + jnp.maximum(i - 1, 0), 0))],
        out_specs=pl.BlockSpec((tq, D_MODEL), lambda b, i: (b * nlat + jnp.maximum(i - 1, 0), 0)),
        out_shape=jax.ShapeDtypeStruct((rows, D_MODEL), BF16),
        scratch_shapes=[pltpu.VMEM((N_KV, M, keys), F32)] * 2 + [pltpu.VMEM((N_KV, M, HEAD_DIM), F32)] * 2,
        compiler_params=_cparams(("arbitrary", "arbitrary")),
        name="attn_window",
    )(sink, q, k, v, k, v, gate)


def _dense_kernel(nlat, q_ref, kl_ref, vl_ref, kc_ref, vc_ref, gate_ref, o_ref, s_a, s_b, m_a, m_b):
    tq = q_ref.shape[0]
    L = kc_ref.shape[0]
    S = kl_ref.shape[0]
    i = pl.program_id(1)
    bounds = [(0, L)] + [(L + j0, L + j0 + KEY_CHUNK) for j0 in range(0, S, KEY_CHUNK)]

    def step(write, read):
        for g in range(N_KV):
            if write is None:
                break
            cs, _ = _kv_slices(g)
            qs = _stack_heads(q_ref, g)
            mrun = None
            for lo, hi in bounds:
                k = kc_ref[:, cs] if lo == 0 else kl_ref[lo - L:hi - L, cs]
                s = _qk(qs, k)
                write[0][g, :, lo:hi] = s
                mrun = _lane_fold(s, mrun, jnp.maximum)
            write[1][g] = _row_shift(mrun)
        for g in range(N_KV):
            if read is None:
                break
            _, vs = _kv_slices(g)
            m = read[1][g]
            o = None
            for lo, hi in bounds:
                v = vc_ref[:, vs] if lo == 0 else vl_ref[lo - L:hi - L, vs]
                d = jnp.dot(_shifted_exp2(read[0][g, :, lo:hi], m), v, preferred_element_type=F32)
                o = d if o is None else o + d
            _store_heads(o_ref, gate_ref, o[:, :HEAD_DIM] * (1.0 / o[:, HEAD_DIM:]), g, tq)

    a, b = (s_a, m_a), (s_b, m_b)
    odd = i % 2 == 1
    steady = jnp.logical_and(i > 0, i < nlat)

    @pl.when(i == 0)
    def _():
        step(a, None)

    @pl.when(jnp.logical_and(steady, odd))
    def _():
        step(b, a)

    @pl.when(jnp.logical_and(steady, jnp.logical_not(odd)))
    def _():
        step(a, b)

    @pl.when(i == nlat)
    def _():
        step(None, b)


def _dense_attention(q, k, v, gate, dims, tq, rows):
    B, S, L = dims
    nlat = S // tq
    assert nlat % 2 == 0
    one = pl.Buffered(1)
    return pl.pallas_call(
        functools.partial(_dense_kernel, nlat),
        grid=(B, nlat + 1),
        in_specs=[
            pl.BlockSpec((tq, D_MODEL), lambda b, i: (b * nlat + jnp.minimum(i, nlat - 1), 0)),
            pl.BlockSpec((S, KV_W), lambda b, i: (b, 0)),
            pl.BlockSpec((S, 2 * KV_W), lambda b, i: (b, 0)),
            pl.BlockSpec((L, KV_W), lambda b, i: (B * S // L + b, 0), pipeline_mode=one),
            pl.BlockSpec((L, 2 * KV_W), lambda b, i: (B * S // L + b, 0), pipeline_mode=one),
            pl.BlockSpec((tq, D_MODEL), lambda b, i: (b * nlat + jnp.maximum(i - 1, 0), 0)),
        ],
        out_specs=pl.BlockSpec((tq, D_MODEL), lambda b, i: (b * nlat + jnp.maximum(i - 1, 0), 0)),
        out_shape=jax.ShapeDtypeStruct((rows, D_MODEL), BF16),
        scratch_shapes=[pltpu.VMEM((N_KV, GROUP * tq, L + S), F32)] * 2
        + [pltpu.VMEM((N_KV, GROUP * tq, HEAD_DIM), F32)] * 2,
        compiler_params=_cparams(("arbitrary", "arbitrary")),
        name="attn_dense",
    )(q, k, v, k, v, gate)


def _ctx_kernel(sink_ref, qb_ref, kb_ref, vb_ref, gb_ref, qc_ref, kc_ref, vc_ref, gc_ref, yb_any, yc_any,
                ob_ref, oc_ref):
    del yb_any, yc_any
    tq = qb_ref.shape[0]
    for g in range(N_KV):
        cs, vs = _kv_slices(g)
        qs = _stack_heads(qb_ref, g)
        ob = _softmax_pv([_qk(qs, kb_ref[:, cs])], [vb_ref[:, vs]], _sink_col(sink_ref, g, tq))
        _store_heads(ob_ref, gb_ref, ob, g, tq)
        qs = _stack_heads(qc_ref, g)
        oc = _softmax_pv([_qk(qs, kc_ref[:, cs])], [vc_ref[:, vs]], None)
        _store_heads(oc_ref, gc_ref, oc, g, tq)


def _ctx_attention(sink, qB, kB, vB, gB, qC, kC, vC, gC, yB, yC, dims):
    B, S, L = dims
    c0 = B * S // L
    rows = lambda wd: pl.BlockSpec((L, wd), lambda b: (c0 + b, 0))
    anyspec = pl.BlockSpec(memory_space=pl.ANY)
    return pl.pallas_call(
        _ctx_kernel,
        grid=(B,),
        in_specs=[pl.BlockSpec(memory_space=pltpu.SMEM),
                  rows(D_MODEL), rows(KV_W), rows(2 * KV_W), rows(D_MODEL),
                  rows(D_MODEL), rows(KV_W), rows(2 * KV_W), rows(D_MODEL),
                  anyspec, anyspec],
        out_specs=[rows(D_MODEL), rows(D_MODEL)],
        out_shape=[jax.ShapeDtypeStruct(yB.shape, BF16), jax.ShapeDtypeStruct(yC.shape, BF16)],
        input_output_aliases={9: 0, 10: 1},
        compiler_params=_cparams(("arbitrary",)),
        name="attn_ctx",
    )(sink, qB, kB, vB, gB, qC, kC, vC, gC, yB, yC)


def _merge_kernel(final, split, x_ref, xc_ref, mod_ref, yf_ref, yr_ref, ga_ref, yb_ref, yc_ref, m_ref,
                  wb_ref, wo_ref, fg_ref, o_ref):
    ya = (yf_ref[...].astype(F32) + yr_ref[...].astype(F32)) * ga_ref[...].astype(F32)
    ts = (ya.astype(BF16), yb_ref[...], yc_ref[...])
    acc = 0.0
    for n in range(3):
        br = jnp.dot(ts[n], wb_ref[n], preferred_element_type=F32)
        acc = acc + m_ref[:, n * D_MODEL:(n + 1) * D_MODEL].astype(F32) * br
    y = jnp.dot(acc.astype(BF16), wo_ref[...], preferred_element_type=F32)
    xn = _rows(split, x_ref, xc_ref) + mod_ref[:, 2 * D_MODEL:] * y
    if final:
        xn = xn * lax.rsqrt(jnp.mean(xn * xn, axis=-1, keepdims=True) + EPS) * fg_ref[...]
    o_ref[...] = xn


def _merge(X, Xc, mod, yf, yr, ga, yb, yc, m, wb, wo, fg, dims, tm, final):
    B, S, L = dims
    n = B * S if final else B * (S + L)
    ts = S // tm
    nlat = B * ts
    x_specs, x_args = _row_source_specs(X, Xc, tm, nlat)

    def mod_idx(i):
        return (jnp.where(i < nlat, i // ts, B), 0, 0)

    rows = lambda wd: pl.BlockSpec((tm, wd), lambda i: (i, 0))
    return pl.pallas_call(
        functools.partial(_merge_kernel, final, None if Xc is None else nlat),
        grid=(n // tm,),
        in_specs=x_specs + [
            pl.BlockSpec((None, 1, 3 * D_MODEL), mod_idx),
            rows(D_MODEL), rows(D_MODEL), rows(D_MODEL), rows(D_MODEL), rows(D_MODEL), rows(3 * D_MODEL),
            pl.BlockSpec((3, D_MODEL, D_MODEL), lambda i: (0, 0, 0), pipeline_mode=pl.Buffered(1)),
            pl.BlockSpec((D_MODEL, D_MODEL), lambda i: (0, 0), pipeline_mode=pl.Buffered(1)),
            pl.BlockSpec((1, D_MODEL), lambda i: (0, 0)),
        ],
        out_specs=rows(D_MODEL),
        out_shape=jax.ShapeDtypeStruct((n, D_MODEL), F32),
        compiler_params=_cparams(("parallel",)),
        name="merge_final" if final else "merge",
    )(*x_args, mod, yf, yr, ga, yb, yc, m, wb, wo, fg)


def _rope_tables(S, tm):
    t = jnp.arange(S)
    P = HEAD_DIM // 4
    inv = ROPE_THETA ** (-jnp.arange(P, dtype=F32) / P)
    ang = jnp.stack([(t // GRID_W)[:, None] * inv, (t % GRID_W)[:, None] * inv], axis=1)
    cos = jnp.cos(ang).reshape(S, HEAD_DIM // 2)
    sin = jnp.sin(ang).reshape(S, HEAD_DIM // 2)
    c = jnp.concatenate([cos, cos], axis=1)
    s = jnp.concatenate([-sin, sin], axis=1)
    ident = jnp.zeros((tm, HEAD_DIM), F32)
    return jnp.concatenate([c, ident + 1.0]), jnp.concatenate([s, ident])


def kernel(x, c, ctx, c_ctx, norm_g, w_mod, b_mod, w_in, conv_w, conv_b, lru_wa, lru_ba, lru_wx, lru_bx,
           lru_lambda, attn_sink, q_norm_g, k_norm_g, w_branch, w_out, final_g):
    B, S, _ = x.shape
    L = ctx.shape[1]
    depth = w_in.shape[0]
    dims = (B, S, L)
    assert S % L == 0 and L % 128 == 0 and S % GRID_W == 0 and S >= 256 + 2 * WINDOW
    tm = math.gcd(1024, math.gcd(S, B * L))
    tm_merge = math.gcd(512, tm)
    tq_w = 256
    tq_g = 128

    X, Xc = x.reshape(B * S, D_MODEL), ctx.reshape(B * L, D_MODEL)
    rows = -(-(B + 1) // 8) * 8
    cc = jnp.concatenate([c, c_ctx[None], jnp.zeros((rows - B - 1, D_MODEL), F32)], axis=0)
    mod_all = _modulation(cc, w_mod, b_mod).reshape(depth, rows, 1, 3 * D_MODEL)
    tabs = _rope_tables(S, tm)
    w_in16 = w_in.astype(BF16)
    cuts = (2 * D_MODEL, 3 * D_MODEL, 3 * D_MODEL + KV_W, 4 * D_MODEL + 2 * KV_W, 5 * D_MODEL + 2 * KV_W,
            5 * D_MODEL + 3 * KV_W)
    w_in16 = jnp.concatenate(
        [w_in16[..., :cuts[0]], _pair_layout(w_in16[..., cuts[0]:cuts[1]]), _pair_layout(w_in16[..., cuts[1]:cuts[2]]),
         w_in16[..., cuts[2]:cuts[3]], _pair_layout(w_in16[..., cuts[3]:cuts[4]]),
         _pair_layout(w_in16[..., cuts[4]:cuts[5]]), w_in16[..., cuts[5]:]], axis=-1)
    wb16 = w_branch.astype(BF16)
    wo16 = w_out.astype(BF16)
    row = lambda a: a.reshape(1, -1)

    a_specs = (("plain", D_MODEL), ("silu", D_MODEL))
    b_specs = (("rope_q", D_MODEL), ("rope_k", KV_W), ("value", KV_W), ("silu", D_MODEL))
    c_specs = (("normrope_q", D_MODEL), ("normrope_k", KV_W), ("value", KV_W), ("silu", D_MODEL))
    m_specs = (("sigmoid", 3 * D_MODEL),)
    a_w = 2 * D_MODEL
    b_w = 2 * D_MODEL + 2 * KV_W

    out = None
    for l in range(depth):
        last = l == depth - 1
        mod = mod_all[l]
        g = row(norm_g[l])
        qg, kg = _pair_layout(row(q_norm_g[l])), _pair_layout(row(k_norm_g[l]))
        w = w_in16[l]
        proj = functools.partial(_inproj, X, Xc, mod, g, tabs=tabs, qg=qg, kg=kg, dims=dims, tm=tm)
        uA, gA = proj(w=w[:, :a_w], specs=a_specs, dtypes=(F32, BF16), name="inproj_a")
        qB, kB, vB, gB = proj(w=w[:, a_w:a_w + b_w], specs=b_specs, dtypes=(BF16,) * 4, name="inproj_b")
        qC, kC, vC, gC = proj(w=w[:, a_w + b_w:a_w + 2 * b_w], specs=c_specs, dtypes=(BF16,) * 4,
                              name="inproj_c")
        (m,) = proj(w=w[:, a_w + 2 * b_w:], specs=m_specs, dtypes=(BF16,), name="inproj_m")

        ys = []
        for d, rev in enumerate((False, True)):
            wg = _gate_weights(lru_wa[l, d], lru_wx[l, d])
            weights = (conv_w[l], row(conv_b[l]), wg, row(lru_ba[l, d]), row(lru_bx[l, d]), row(lru_lambda[l, d]))
            ys.append(_lru(uA, weights, dims, rev, "lru_rev" if rev else "lru_fwd"))
        rows_out = B * S if last else B * (S + L)
        yB = _window_attention(qB, kB, vB, gB, attn_sink[l], dims, tq_w, rows_out)
        yC = _dense_attention(qC, kC, vC, gC, dims, tq_g, rows_out)
        if not last:
            yB, yC = _ctx_attention(attn_sink[l], qB, kB, vB, gB, qC, kC, vC, gC, yB, yC, dims)
        res = _merge(X, Xc, mod, ys[0], ys[1], gA, yB, yC, m, wb16[l], wo16[l], row(final_g), dims, tm_merge,
                     last)
        if last:
            out = res.reshape(B, S, D_MODEL)
        else:
            X, Xc = res, None
    return out
```

```python
import functools
import math

import jax
import jax.numpy as jnp
from jax import lax
from jax.experimental import pallas as pl
from jax.experimental.pallas import tpu as pltpu

D_MODEL = 1024
HEAD_DIM = 128
N_HEADS = D_MODEL // HEAD_DIM
N_KV = N_HEADS // 4
GROUP = N_HEADS // N_KV
KV_W = N_KV * HEAD_DIM
LRU_BLOCK_W = 64
LRU_C = 8.0
CONV_W = 4
CONV_LEFT = 2
WINDOW = 128
GRID_W = 64
ROPE_THETA = 10000.0
EPS = 1e-6
NEG_INF = -1e30
LOG2E = 1.4426950408889634
Q_SCALE = HEAD_DIM ** -0.5 * LOG2E
GATE_CHUNK = 256
KEY_CHUNK = 512
LRU_BATCH = 4
VMEM_LIMIT = 56 * 1024 * 1024

F32 = jnp.float32
BF16 = jnp.bfloat16


def _cparams(sem):
    return pltpu.CompilerParams(dimension_semantics=sem, vmem_limit_bytes=VMEM_LIMIT)


def _sigmoid(x):
    return 0.5 * (jnp.tanh(0.5 * x) + 1.0)


def _mod_kernel(cc_ref, w_ref, b_ref, o_ref):
    cc = cc_ref[...]
    s = cc * _sigmoid(cc)
    o_ref[...] = jnp.dot(s, w_ref[...], preferred_element_type=F32,
                         precision=lax.Precision.HIGHEST) + b_ref[...]


def _modulation(cc, w_mod, b_mod):
    depth = w_mod.shape[0]
    rows = cc.shape[0]
    return pl.pallas_call(
        _mod_kernel,
        grid=(depth, 3),
        in_specs=[
            pl.BlockSpec((rows, D_MODEL), lambda l, j: (0, 0)),
            pl.BlockSpec((None, D_MODEL, D_MODEL), lambda l, j: (l, 0, j)),
            pl.BlockSpec((None, 1, D_MODEL), lambda l, j: (l, 0, j)),
        ],
        out_specs=pl.BlockSpec((None, rows, D_MODEL), lambda l, j: (l, 0, j)),
        out_shape=jax.ShapeDtypeStruct((depth, rows, 3 * D_MODEL), F32),
        compiler_params=_cparams(("arbitrary", "arbitrary")),
        name="modulation",
    )(cc, w_mod, b_mod.reshape(depth, 1, 3 * D_MODEL))


def _norm_mod(x, g, mod):
    ms = jnp.mean(x * x, axis=-1, keepdims=True)
    y = x * lax.rsqrt(ms + EPS) * g
    return y * (1.0 + mod[:, D_MODEL:2 * D_MODEL]) + mod[:, :D_MODEL]


def _pair_layout(w):
    lead = w.shape[:-1]
    nh = w.shape[-1] // HEAD_DIM
    return w.reshape(*lead, nh, 2, 2, HEAD_DIM // 4).swapaxes(-3, -2).reshape(*lead, nh * HEAD_DIM)


def _rope(yh, c, s):
    return yh * c + pltpu.roll(yh, HEAD_DIM // 2, 1) * s


def _head_rms_scale(y):
    two = 2 * HEAD_DIM
    rows = lax.broadcasted_iota(jnp.int32, (two, two), 0) // HEAD_DIM
    cols = lax.broadcasted_iota(jnp.int32, (two, two), 1) // HEAD_DIM
    avg = jnp.where(rows == cols, 1.0 / HEAD_DIM, 0.0).astype(BF16)
    sq = (y * y).astype(BF16)
    ms = [jnp.dot(sq[:, p * two:(p + 1) * two], avg, preferred_element_type=F32) for p in range(y.shape[1] // two)]
    return lax.rsqrt(jnp.concatenate(ms, axis=1) + EPS)


def _row_source_specs(X, Xc, tm, nlat):
    if Xc is None:
        return [pl.BlockSpec((tm, D_MODEL), lambda i: (i, 0)), pl.BlockSpec((tm, D_MODEL), lambda i: (0, 0))], (X, X)
    return [pl.BlockSpec((tm, D_MODEL), lambda i: (jnp.minimum(i, nlat - 1), 0)),
            pl.BlockSpec((tm, D_MODEL), lambda i: (jnp.maximum(i - nlat, 0), 0))], (X, Xc)


def _rows(split, x_ref, xc_ref):
    if split is None:
        return x_ref[...]
    return jnp.where(pl.program_id(0) < split, x_ref[...], xc_ref[...])


def _inproj_kernel(specs, split, x_ref, xc_ref, mod_ref, g_ref, w_ref, c_ref, s_ref, qg_ref, kg_ref, *out_refs):
    h = _norm_mod(_rows(split, x_ref, xc_ref), g_ref[...], mod_ref[...]).astype(BF16)
    off = 0
    for (kind, width), o_ref in zip(specs, out_refs):
        y = jnp.dot(h, w_ref[:, off:off + width], preferred_element_type=F32)
        off += width
        if kind == "plain":
            o_ref[...] = y.astype(o_ref.dtype)
        elif kind == "value":
            for hh in range(width // HEAD_DIM):
                o_ref[:, 2 * hh * HEAD_DIM:(2 * hh + 1) * HEAD_DIM] = (
                    y[:, hh * HEAD_DIM:(hh + 1) * HEAD_DIM].astype(o_ref.dtype))
                o_ref[:, (2 * hh + 1) * HEAD_DIM:(2 * hh + 2) * HEAD_DIM] = jnp.ones(
                    (y.shape[0], HEAD_DIM), o_ref.dtype)
        elif kind == "silu":
            o_ref[...] = (y * _sigmoid(y)).astype(o_ref.dtype)
        elif kind == "sigmoid":
            o_ref[...] = _sigmoid(y).astype(o_ref.dtype)
        else:
            is_q = kind in ("rope_q", "normrope_q")
            norm = kind in ("normrope_q", "normrope_k")
            c, s = c_ref[...], s_ref[...]
            scale = _head_rms_scale(y) if norm else None
            for hh in range(width // HEAD_DIM):
                sl = slice(hh * HEAD_DIM, (hh + 1) * HEAD_DIM)
                yh = y[:, sl]
                if norm:
                    yh = yh * scale[:, sl] * (qg_ref[...] if is_q else kg_ref[...])
                yh = _rope(yh, c, s)
                if is_q:
                    yh = yh * Q_SCALE
                o_ref[:, sl] = yh.astype(o_ref.dtype)


def _inproj(X, Xc, mod, g, w, tabs, qg, kg, specs, dtypes, dims, tm, name):
    B, S, L = dims
    n = B * (S + L)
    ts = S // tm
    nlat = B * ts
    wtot = w.shape[1]

    def mod_idx(i):
        return (jnp.where(i < nlat, i // ts, B), 0, 0)

    def tab_idx(i):
        return (jnp.where(i < nlat, i % ts, ts), 0)

    widths = [2 * wd if kind == "value" else wd for kind, wd in specs]
    out_shape = [jax.ShapeDtypeStruct((n, wd), dt) for wd, dt in zip(widths, dtypes)]
    out_specs = [pl.BlockSpec((tm, wd), lambda i: (i, 0)) for wd in widths]
    tab_spec = pl.BlockSpec((tm, HEAD_DIM), tab_idx)
    x_specs, x_args = _row_source_specs(X, Xc, tm, nlat)
    return pl.pallas_call(
        functools.partial(_inproj_kernel, specs, None if Xc is None else nlat),
        grid=(n // tm,),
        in_specs=x_specs + [
            pl.BlockSpec((None, 1, 3 * D_MODEL), mod_idx),
            pl.BlockSpec((1, D_MODEL), lambda i: (0, 0)),
            pl.BlockSpec((D_MODEL, wtot), lambda i: (0, 0)),
            tab_spec, tab_spec,
            pl.BlockSpec((1, HEAD_DIM), lambda i: (0, 0)),
            pl.BlockSpec((1, HEAD_DIM), lambda i: (0, 0)),
        ],
        out_specs=out_specs,
        out_shape=out_shape,
        compiler_params=_cparams(("parallel",)),
        name=name,
    )(*x_args, mod, g, w, tabs[0], tabs[1], qg, kg)


def _lru_kernel(reverse, nchunk, u_ref, up_ref, un_ref, h0_ref, cw_ref, cb_ref, wg_ref, ba_ref, bx_ref, lam_ref,
                *rest):
    y_ref, hout_ref, ext_s, a_s, b_s, h_s = rest[-6:]
    K, T, _ = u_ref.shape
    c = pl.program_id(1)
    j = (nchunk - 1 - c) if reverse else c
    has_prev = j > 0
    has_next = j < nchunk - 1

    lam = lam_ref[...]
    neg = -lam
    softplus = jnp.maximum(neg, 0.0) + jnp.log1p(jnp.exp(-jnp.abs(neg)))
    c1 = (-0.5 * LRU_C * LOG2E) * softplus
    hba = 0.5 * ba_ref[...]
    hbx = 0.5 * bx_ref[...]
    cw = 0.5 * cw_ref[...]
    cb = 0.5 * cb_ref[...]

    for kb in range(K):
        ext_s[kb, 0:8, :] = jnp.where(has_prev, up_ref[kb], 0.0)
        ext_s[kb, 8:8 + T, :] = u_ref[kb]
        ext_s[kb, 8 + T:16 + T, :] = jnp.where(has_next, un_ref[kb], 0.0)
        ext = ext_s[kb]
        hu = cb
        for k in range(CONV_W):
            shift = (CONV_LEFT - k) % (T + 16)
            tap = ext if shift == 0 else pltpu.roll(ext, shift, 0)
            hu = hu + tap[8:8 + T, :] * cw[k:k + 1, :]
        ub = hu.astype(BF16)
        for q in range(D_MODEL // GATE_CHUNK):
            sl = slice(q * GATE_CHUNK, (q + 1) * GATE_CHUNK)
            z = jnp.dot(ub[:, sl], wg_ref[q], preferred_element_type=F32)
            tr = jnp.tanh(z[:, :GATE_CHUNK] + hba[:, sl])
            ti = jnp.tanh(z[:, GATE_CHUNK:] + hbx[:, sl])
            a = jnp.exp2(c1[:, sl] * tr + c1[:, sl])
            x = 1.0 - a * a
            root = jnp.where(x > 0.0, x * lax.rsqrt(x), 0.0)
            b = root * ((ti + 1.0) * hu[:, sl])
            for n in range(GATE_CHUNK // HEAD_DIM):
                lanes = slice(n * HEAD_DIM, (n + 1) * HEAD_DIM)
                sub = q * (GATE_CHUNK // HEAD_DIM) + n
                a_s[kb, pl.ds(sub, T, stride=8), :] = a[:, lanes]
                b_s[kb, pl.ds(sub, T, stride=8), :] = b[:, lanes]

    @pl.when(c == 0)
    def _():
        h_s[...] = h0_ref[...]

    def body(t, hs):
        tt = (T - 1 - t) if reverse else t
        rows = pl.ds(pl.multiple_of(tt * 8, 8), 8)
        out = []
        for kb in range(K):
            h = a_s[kb, rows, :] * hs[kb] + b_s[kb, rows, :]
            b_s[kb, rows, :] = h
            out.append(h)
        return tuple(out)

    hs = lax.fori_loop(0, T, body, tuple(h_s[kb] for kb in range(K)), unroll=8)
    for kb in range(K):
        h_s[kb] = hs[kb]
        for n in range(D_MODEL // HEAD_DIM):
            y_ref[kb, :, n * HEAD_DIM:(n + 1) * HEAD_DIM] = b_s[kb, pl.ds(n, T, stride=8), :].astype(y_ref.dtype)

    @pl.when(c == nchunk - 1)
    def _():
        hout_ref[...] = h_s[...]


def _lru_call(u3, h0, y_prev, weights, K, T, seg0, reverse, name):
    rows = u3.shape[1]
    nchunk = rows // T
    t8 = T // 8
    blk0 = seg0 // K

    def chunk(c):
        return (nchunk - 1 - c) if reverse else c

    main = pl.BlockSpec((K, T, D_MODEL), lambda g, c: (blk0 + g, chunk(c), 0))
    prev = pl.BlockSpec((K, 8, D_MODEL), lambda g, c: (blk0 + g, jnp.maximum(chunk(c) * t8 - 1, 0), 0))
    nxt = pl.BlockSpec((K, 8, D_MODEL),
                       lambda g, c: (blk0 + g, jnp.minimum((chunk(c) + 1) * t8, rows // 8 - 1), 0))
    hspec = pl.BlockSpec((None, K, 8, HEAD_DIM), lambda g, c: (g, 0, 0, 0))
    fixed = lambda shape: pl.BlockSpec(shape, lambda g, c: (0,) * len(shape))
    in_specs = [main, prev, nxt, hspec, fixed((CONV_W, D_MODEL)), fixed((1, D_MODEL)),
                fixed((D_MODEL // GATE_CHUNK, GATE_CHUNK, 2 * GATE_CHUNK)),
                fixed((1, D_MODEL)), fixed((1, D_MODEL)), fixed((1, D_MODEL))]
    args = [u3, u3, u3, h0, *weights]
    aliases = {}
    if y_prev is not None:
        in_specs.append(pl.BlockSpec(memory_space=pl.ANY))
        args.append(y_prev)
        aliases = {len(args) - 1: 0}
    return pl.pallas_call(
        functools.partial(_lru_kernel, reverse, nchunk),
        grid=(h0.shape[0], nchunk),
        in_specs=in_specs,
        out_specs=[main, hspec],
        out_shape=[jax.ShapeDtypeStruct(u3.shape, BF16), jax.ShapeDtypeStruct(h0.shape, F32)],
        input_output_aliases=aliases,
        scratch_shapes=[
            pltpu.VMEM((K, T + 16, D_MODEL), F32),
            pltpu.VMEM((K, T * 8, HEAD_DIM), F32),
            pltpu.VMEM((K, T * 8, HEAD_DIM), F32),
            pltpu.VMEM((K, 8, HEAD_DIM), F32),
        ],
        compiler_params=_cparams(("arbitrary", "arbitrary")),
        name=name,
    )(*args)


def _lru(uA, weights, dims, reverse, name):
    B, S, L = dims
    n = uA.shape[0]
    K = LRU_BATCH
    assert B % K == 0 and n % S == 0 and (B * S // L) % K == 0
    assert D_MODEL == 8 * HEAD_DIM
    h0 = jnp.zeros((B // K, K, 8, HEAD_DIM), F32)
    y, h = _lru_call(uA.reshape(n // L, L, D_MODEL), h0, None, weights, K, L, B * S // L, reverse, name + "_ctx")
    y, _ = _lru_call(uA.reshape(n // S, S, D_MODEL), h, y.reshape(n // S, S, D_MODEL), weights, K, L, 0, reverse,
                     name)
    return y.reshape(n, D_MODEL)


def _gate_weights(wa, wx):
    per = GATE_CHUNK // LRU_BLOCK_W
    nchunks = wa.shape[0] // per
    eye = jnp.eye(per, dtype=wa.dtype)

    def bd(w):
        w4 = w.reshape(nchunks, per, LRU_BLOCK_W, LRU_BLOCK_W)
        return jnp.einsum("jkde,kl->jkdle", w4, eye).reshape(nchunks, GATE_CHUNK, GATE_CHUNK)

    return jnp.concatenate([bd(wa), bd(wx)], axis=-1).astype(BF16)


def _row_shift(mrun):
    mb = mrun.astype(BF16)
    return jnp.broadcast_to(mb.max(axis=-1, keepdims=True), mb.shape).astype(F32)


def _shifted_exp2(s, shift):
    blocks = [jnp.exp2(s[:, c * HEAD_DIM:(c + 1) * HEAD_DIM] - shift) for c in range(s.shape[1] // HEAD_DIM)]
    return jnp.concatenate(blocks, axis=1).astype(BF16)


def _lane_fold(x, acc, op):
    for c in range(x.shape[1] // HEAD_DIM):
        blk = x[:, c * HEAD_DIM:(c + 1) * HEAD_DIM]
        acc = blk if acc is None else op(acc, blk)
    return acc


def _softmax_pv(scores, values, sink):
    mrun = None
    for s in scores:
        mrun = _lane_fold(s, mrun, jnp.maximum)
    m = mrun.max(axis=-1, keepdims=True)
    if sink is not None:
        m = jnp.maximum(m, sink)
    o = None
    for s, v in zip(scores, values):
        d = jnp.dot(jnp.exp2(s - m).astype(BF16), v, preferred_element_type=F32)
        o = d if o is None else o + d
    l = o[:, HEAD_DIM:]
    if sink is not None:
        l = l + jnp.exp2(sink - m)
    return o[:, :HEAD_DIM] * (1.0 / l)


def _qk(q, k):
    return lax.dot_general(q, k, (((1,), (1,)), ((), ())), preferred_element_type=F32)


def _stack_heads(q_ref, g):
    return jnp.concatenate(
        [q_ref[:, (g * GROUP + h) * HEAD_DIM:(g * GROUP + h + 1) * HEAD_DIM] for h in range(GROUP)], axis=0)


def _sink_col(sink_ref, g, tq):
    return jnp.concatenate(
        [jnp.full((tq, 1), sink_ref[g * GROUP + h] * LOG2E, F32) for h in range(GROUP)], axis=0)


def _sink_lanes(sink_ref, g, tq):
    return jnp.concatenate(
        [jnp.full((tq, HEAD_DIM), sink_ref[g * GROUP + h] * LOG2E, F32) for h in range(GROUP)], axis=0)


def _store_heads(o_ref, gate_ref, o, g, tq):
    for h in range(GROUP):
        sl = slice((g * GROUP + h) * HEAD_DIM, (g * GROUP + h + 1) * HEAD_DIM)
        o_ref[:, sl] = (o[h * tq:(h + 1) * tq] * gate_ref[:, sl].astype(F32)).astype(o_ref.dtype)


def _kv_slices(g):
    return slice(g * HEAD_DIM, (g + 1) * HEAD_DIM), slice(2 * g * HEAD_DIM, 2 * (g + 1) * HEAD_DIM)


def _window_kernel(nlat, S, sink_ref, q_ref, kl_ref, vl_ref, kc_ref, vc_ref, gate_ref, o_ref, s_a, s_b, m_a, m_b):
    tq = q_ref.shape[0]
    L = kc_ref.shape[0]
    span = tq + 2 * WINDOW
    i = pl.program_id(1)

    def band_start(tile):
        return pl.multiple_of(jnp.clip(tile * tq - WINDOW, 0, S - span), HEAD_DIM)

    def step(write, read):
        if write is not None:
            start = band_start(i)
            qpos = i * tq + lax.broadcasted_iota(jnp.int32, (tq, span), 0)
            kpos = start + lax.broadcasted_iota(jnp.int32, (tq, span), 1)
            bias = jnp.where(jnp.abs(kpos - qpos) <= WINDOW, 0.0, NEG_INF)
            bias = jnp.concatenate([bias] * GROUP, axis=0)
            for g in range(N_KV):
                cs, _ = _kv_slices(g)
                qs = _stack_heads(q_ref, g)
                s_ctx = _qk(qs, kc_ref[:, cs])
                s_lat = _qk(qs, kl_ref[pl.ds(start, span), cs]) + bias
                write[0][g, :, :L] = s_ctx
                write[0][g, :, L:] = s_lat
                mrun = _lane_fold(s_lat, _lane_fold(s_ctx, None, jnp.maximum), jnp.maximum)
                write[1][g] = jnp.maximum(_row_shift(mrun), _sink_lanes(sink_ref, g, tq))
        if read is not None:
            start = band_start(i - 1)
            for g in range(N_KV):
                _, vs = _kv_slices(g)
                m = read[1][g]
                o = jnp.dot(_shifted_exp2(read[0][g, :, :L], m), vc_ref[:, vs], preferred_element_type=F32)
                o = o + jnp.dot(_shifted_exp2(read[0][g, :, L:], m), vl_ref[pl.ds(start, span), vs],
                                preferred_element_type=F32)
                l = o[:, HEAD_DIM:] + jnp.exp2(_sink_lanes(sink_ref, g, tq) - m)
                _store_heads(o_ref, gate_ref, o[:, :HEAD_DIM] * (1.0 / l), g, tq)

    a, b = (s_a, m_a), (s_b, m_b)
    odd = i % 2 == 1
    steady = jnp.logical_and(i > 0, i < nlat)

    @pl.when(i == 0)
    def _():
        step(a, None)

    @pl.when(jnp.logical_and(steady, odd))
    def _():
        step(b, a)

    @pl.when(jnp.logical_and(steady, jnp.logical_not(odd)))
    def _():
        step(a, b)

    @pl.when(i == nlat)
    def _():
        step(None, b)


def _window_attention(q, k, v, gate, sink, dims, tq, rows):
    B, S, L = dims
    nlat = S // tq
    assert nlat % 2 == 0
    M = GROUP * tq
    keys = L + tq + 2 * WINDOW
    lat_spec = lambda wd: pl.BlockSpec((S, wd), lambda b, i: (b, 0))
    ctx_spec = lambda wd: pl.BlockSpec((L, wd), lambda b, i: (B * S // L + b, 0))
    return pl.pallas_call(
        functools.partial(_window_kernel, nlat, S),
        grid=(B, nlat + 1),
        in_specs=[pl.BlockSpec(memory_space=pltpu.SMEM),
                  pl.BlockSpec((tq, D_MODEL), lambda b, i: (b * nlat + jnp.minimum(i, nlat - 1), 0)),
                  lat_spec(KV_W), lat_spec(2 * KV_W), ctx_spec(KV_W), ctx_spec(2 * KV_W),
                  pl.BlockSpec((tq, D_MODEL), lambda b, i: (b * nlat + jnp.maximum(i - 1, 0), 0))],
        out_specs=pl.BlockSpec((tq, D_MODEL), lambda b, i: (b * nlat + jnp.maximum(i - 1, 0), 0)),
        out_shape=jax.ShapeDtypeStruct((rows, D_MODEL), BF16),
        scratch_shapes=[pltpu.VMEM((N_KV, M, keys), F32)] * 2 + [pltpu.VMEM((N_KV, M, HEAD_DIM), F32)] * 2,
        compiler_params=_cparams(("arbitrary", "arbitrary")),
        name="attn_window",
    )(sink, q, k, v, k, v, gate)


def _dense_kernel(nlat, q_ref, kl_ref, vl_ref, kc_ref, vc_ref, gate_ref, o_ref, s_a, s_b, m_a, m_b):
    tq = q_ref.shape[0]
    L = kc_ref.shape[0]
    S = kl_ref.shape[0]
    i = pl.program_id(1)
    bounds = [(0, L)] + [(L + j0, L + j0 + KEY_CHUNK) for j0 in range(0, S, KEY_CHUNK)]

    def step(write, read):
        for g in range(N_KV):
            if write is None:
                break
            cs, _ = _kv_slices(g)
            qs = _stack_heads(q_ref, g)
            mrun = None
            for lo, hi in bounds:
                k = kc_ref[:, cs] if lo == 0 else kl_ref[lo - L:hi - L, cs]
                s = _qk(qs, k)
                write[0][g, :, lo:hi] = s
                mrun = _lane_fold(s, mrun, jnp.maximum)
            write[1][g] = _row_shift(mrun)
        for g in range(N_KV):
            if read is None:
                break
            _, vs = _kv_slices(g)
            m = read[1][g]
            o = None
            for lo, hi in bounds:
                v = vc_ref[:, vs] if lo == 0 else vl_ref[lo - L:hi - L, vs]
                d = jnp.dot(_shifted_exp2(read[0][g, :, lo:hi], m), v, preferred_element_type=F32)
                o = d if o is None else o + d
            _store_heads(o_ref, gate_ref, o[:, :HEAD_DIM] * (1.0 / o[:, HEAD_DIM:]), g, tq)

    a, b = (s_a, m_a), (s_b, m_b)
    odd = i % 2 == 1
    steady = jnp.logical_and(i > 0, i < nlat)

    @pl.when(i == 0)
    def _():
        step(a, None)

    @pl.when(jnp.logical_and(steady, odd))
    def _():
        step(b, a)

    @pl.when(jnp.logical_and(steady, jnp.logical_not(odd)))
    def _():
        step(a, b)

    @pl.when(i == nlat)
    def _():
        step(None, b)


def _dense_attention(q, k, v, gate, dims, tq, rows):
    B, S, L = dims
    nlat = S // tq
    assert nlat % 2 == 0
    one = pl.Buffered(1)
    return pl.pallas_call(
        functools.partial(_dense_kernel, nlat),
        grid=(B, nlat + 1),
        in_specs=[
            pl.BlockSpec((tq, D_MODEL), lambda b, i: (b * nlat + jnp.minimum(i, nlat - 1), 0)),
            pl.BlockSpec((S, KV_W), lambda b, i: (b, 0)),
            pl.BlockSpec((S, 2 * KV_W), lambda b, i: (b, 0)),
            pl.BlockSpec((L, KV_W), lambda b, i: (B * S // L + b, 0), pipeline_mode=one),
            pl.BlockSpec((L, 2 * KV_W), lambda b, i: (B * S // L + b, 0), pipeline_mode=one),
            pl.BlockSpec((tq, D_MODEL), lambda b, i: (b * nlat + jnp.maximum(i - 1, 0), 0)),
        ],
        out_specs=pl.BlockSpec((tq, D_MODEL), lambda b, i: (b * nlat + jnp.maximum(i - 1, 0), 0)),
        out_shape=jax.ShapeDtypeStruct((rows, D_MODEL), BF16),
        scratch_shapes=[pltpu.VMEM((N_KV, GROUP * tq, L + S), F32)] * 2
        + [pltpu.VMEM((N_KV, GROUP * tq, HEAD_DIM), F32)] * 2,
        compiler_params=_cparams(("arbitrary", "arbitrary")),
        name="attn_dense",
    )(q, k, v, k, v, gate)


def _ctx_kernel(sink_ref, qb_ref, kb_ref, vb_ref, gb_ref, qc_ref, kc_ref, vc_ref, gc_ref, yb_any, yc_any,
                ob_ref, oc_ref):
    del yb_any, yc_any
    tq = qb_ref.shape[0]
    for g in range(N_KV):
        cs, vs = _kv_slices(g)
        qs = _stack_heads(qb_ref, g)
        ob = _softmax_pv([_qk(qs, kb_ref[:, cs])], [vb_ref[:, vs]], _sink_col(sink_ref, g, tq))
        _store_heads(ob_ref, gb_ref, ob, g, tq)
        qs = _stack_heads(qc_ref, g)
        oc = _softmax_pv([_qk(qs, kc_ref[:, cs])], [vc_ref[:, vs]], None)
        _store_heads(oc_ref, gc_ref, oc, g, tq)


def _ctx_attention(sink, qB, kB, vB, gB, qC, kC, vC, gC, yB, yC, dims):
    B, S, L = dims
    c0 = B * S // L
    rows = lambda wd: pl.BlockSpec((L, wd), lambda b: (c0 + b, 0))
    anyspec = pl.BlockSpec(memory_space=pl.ANY)
    return pl.pallas_call(
        _ctx_kernel,
        grid=(B,),
        in_specs=[pl.BlockSpec(memory_space=pltpu.SMEM),
                  rows(D_MODEL), rows(KV_W), rows(2 * KV_W), rows(D_MODEL),
                  rows(D_MODEL), rows(KV_W), rows(2 * KV_W), rows(D_MODEL),
                  anyspec, anyspec],
        out_specs=[rows(D_MODEL), rows(D_MODEL)],
        out_shape=[jax.ShapeDtypeStruct(yB.shape, BF16), jax.ShapeDtypeStruct(yC.shape, BF16)],
        input_output_aliases={9: 0, 10: 1},
        compiler_params=_cparams(("arbitrary",)),
        name="attn_ctx",
    )(sink, qB, kB, vB, gB, qC, kC, vC, gC, yB, yC)


def _merge_kernel(final, split, x_ref, xc_ref, mod_ref, yf_ref, yr_ref, ga_ref, yb_ref, yc_ref, m_ref,
                  wb_ref, wo_ref, fg_ref, o_ref):
    ya = (yf_ref[...].astype(F32) + yr_ref[...].astype(F32)) * ga_ref[...].astype(F32)
    ts = (ya.astype(BF16), yb_ref[...], yc_ref[...])
    acc = 0.0
    for n in range(3):
        br = jnp.dot(ts[n], wb_ref[n], preferred_element_type=F32)
        acc = acc + m_ref[:, n * D_MODEL:(n + 1) * D_MODEL].astype(F32) * br
    y = jnp.dot(acc.astype(BF16), wo_ref[...], preferred_element_type=F32)
    xn = _rows(split, x_ref, xc_ref) + mod_ref[:, 2 * D_MODEL:] * y
    if final:
        xn = xn * lax.rsqrt(jnp.mean(xn * xn, axis=-1, keepdims=True) + EPS) * fg_ref[...]
    o_ref[...] = xn


def _merge(X, Xc, mod, yf, yr, ga, yb, yc, m, wb, wo, fg, dims, tm, final):
    B, S, L = dims
    n = B * S if final else B * (S + L)
    ts = S // tm
    nlat = B * ts
    x_specs, x_args = _row_source_specs(X, Xc, tm, nlat)

    def mod_idx(i):
        return (jnp.where(i < nlat, i // ts, B), 0, 0)

    rows = lambda wd: pl.BlockSpec((tm, wd), lambda i: (i, 0))
    return pl.pallas_call(
        functools.partial(_merge_kernel, final, None if Xc is None else nlat),
        grid=(n // tm,),
        in_specs=x_specs + [
            pl.BlockSpec((None, 1, 3 * D_MODEL), mod_idx),
            rows(D_MODEL), rows(D_MODEL), rows(D_MODEL), rows(D_MODEL), rows(D_MODEL), rows(3 * D_MODEL),
            pl.BlockSpec((3, D_MODEL, D_MODEL), lambda i: (0, 0, 0), pipeline_mode=pl.Buffered(1)),
            pl.BlockSpec((D_MODEL, D_MODEL), lambda i: (0, 0), pipeline_mode=pl.Buffered(1)),
            pl.BlockSpec((1, D_MODEL), lambda i: (0, 0)),
        ],
        out_specs=rows(D_MODEL),
        out_shape=jax.ShapeDtypeStruct((n, D_MODEL), F32),
        compiler_params=_cparams(("parallel",)),
        name="merge_final" if final else "merge",
    )(*x_args, mod, yf, yr, ga, yb, yc, m, wb, wo, fg)


def _rope_tables(S, tm):
    t = jnp.arange(S)
    P = HEAD_DIM // 4
    inv = ROPE_THETA ** (-jnp.arange(P, dtype=F32) / P)
    ang = jnp.stack([(t // GRID_W)[:, None] * inv, (t % GRID_W)[:, None] * inv], axis=1)
    cos = jnp.cos(ang).reshape(S, HEAD_DIM // 2)
    sin = jnp.sin(ang).reshape(S, HEAD_DIM // 2)
    c = jnp.concatenate([cos, cos], axis=1)
    s = jnp.concatenate([-sin, sin], axis=1)
    ident = jnp.zeros((tm, HEAD_DIM), F32)
    return jnp.concatenate([c, ident + 1.0]), jnp.concatenate([s, ident])


def kernel(x, c, ctx, c_ctx, norm_g, w_mod, b_mod, w_in, conv_w, conv_b, lru_wa, lru_ba, lru_wx, lru_bx,
           lru_lambda, attn_sink, q_norm_g, k_norm_g, w_branch, w_out, final_g):
    B, S, _ = x.shape
    L = ctx.shape[1]
    depth = w_in.shape[0]
    dims = (B, S, L)
    assert S % L == 0 and L % 128 == 0 and S % GRID_W == 0 and S >= 256 + 2 * WINDOW
    tm = math.gcd(1024, math.gcd(S, B * L))
    tm_merge = math.gcd(512, tm)
    tq_w = 256
    tq_g = 128

    X, Xc = x.reshape(B * S, D_MODEL), ctx.reshape(B * L, D_MODEL)
    rows = -(-(B + 1) // 8) * 8
    cc = jnp.concatenate([c, c_ctx[None], jnp.zeros((rows - B - 1, D_MODEL), F32)], axis=0)
    mod_all = _modulation(cc, w_mod, b_mod).reshape(depth, rows, 1, 3 * D_MODEL)
    tabs = _rope_tables(S, tm)
    w_in16 = w_in.astype(BF16)
    cuts = (2 * D_MODEL, 3 * D_MODEL, 3 * D_MODEL + KV_W, 4 * D_MODEL + 2 * KV_W, 5 * D_MODEL + 2 * KV_W,
            5 * D_MODEL + 3 * KV_W)
    w_in16 = jnp.concatenate(
        [w_in16[..., :cuts[0]], _pair_layout(w_in16[..., cuts[0]:cuts[1]]), _pair_layout(w_in16[..., cuts[1]:cuts[2]]),
         w_in16[..., cuts[2]:cuts[3]], _pair_layout(w_in16[..., cuts[3]:cuts[4]]),
         _pair_layout(w_in16[..., cuts[4]:cuts[5]]), w_in16[..., cuts[5]:]], axis=-1)
    wb16 = w_branch.astype(BF16)
    wo16 = w_out.astype(BF16)
    row = lambda a: a.reshape(1, -1)

    a_specs = (("plain", D_MODEL), ("silu", D_MODEL))
    b_specs = (("rope_q", D_MODEL), ("rope_k", KV_W), ("value", KV_W), ("silu", D_MODEL))
    c_specs = (("normrope_q", D_MODEL), ("normrope_k", KV_W), ("value", KV_W), ("silu", D_MODEL))
    m_specs = (("sigmoid", 3 * D_MODEL),)
    a_w = 2 * D_MODEL
    b_w = 2 * D_MODEL + 2 * KV_W

    out = None
    for l in range(depth):
        last = l == depth - 1
        mod = mod_all[l]
        g = row(norm_g[l])
        qg, kg = _pair_layout(row(q_norm_g[l])), _pair_layout(row(k_norm_g[l]))
        w = w_in16[l]
        proj = functools.partial(_inproj, X, Xc, mod, g, tabs=tabs, qg=qg, kg=kg, dims=dims, tm=tm)
        uA, gA = proj(w=w[:, :a_w], specs=a_specs, dtypes=(F32, BF16), name="inproj_a")
        qB, kB, vB, gB = proj(w=w[:, a_w:a_w + b_w], specs=b_specs, dtypes=(BF16,) * 4, name="inproj_b")
        qC, kC, vC, gC = proj(w=w[:, a_w + b_w:a_w + 2 * b_w], specs=c_specs, dtypes=(BF16,) * 4,
                              name="inproj_c")
        (m,) = proj(w=w[:, a_w + 2 * b_w:], specs=m_specs, dtypes=(BF16,), name="inproj_m")

        ys = []
        for d, rev in enumerate((False, True)):
            wg = _gate_weights(lru_wa[l, d], lru_wx[l, d])
            weights = (conv_w[l], row(conv_b[l]), wg, row(lru_ba[l, d]), row(lru_bx[l, d]), row(lru_lambda[l, d]))
            ys.append(_lru(uA, weights, dims, rev, "lru_rev" if rev else "lru_fwd"))
        rows_out = B * S if last else B * (S + L)
        yB = _window_attention(qB, kB, vB, gB, attn_sink[l], dims, tq_w, rows_out)
        yC = _dense_attention(qC, kC, vC, gC, dims, tq_g, rows_out)
        if not last:
            yB, yC = _ctx_attention(attn_sink[l], qB, kB, vB, gB, qC, kC, vC, gC, yB, yC, dims)
        res = _merge(X, Xc, mod, ys[0], ys[1], gA, yB, yC, m, wb16[l], wo16[l], row(final_g), dims, tm_merge,
                     last)
        if last:
            out = res.reshape(B, S, D_MODEL)
        else:
            X, Xc = res, None
    return out
```

```python
import functools
import math

import jax
import jax.numpy as jnp
from jax import lax
from jax.experimental import pallas as pl
from jax.experimental.pallas import tpu as pltpu

D_MODEL = 1024
HEAD_DIM = 128
N_HEADS = D_MODEL // HEAD_DIM
N_KV = N_HEADS // 4
GROUP = N_HEADS // N_KV
KV_W = N_KV * HEAD_DIM
LRU_BLOCK_W = 64
LRU_C = 8.0
CONV_W = 4
CONV_LEFT = 2
WINDOW = 128
GRID_W = 64
ROPE_THETA = 10000.0
EPS = 1e-6
NEG_INF = -1e30
LOG2E = 1.4426950408889634
Q_SCALE = HEAD_DIM ** -0.5 * LOG2E
GATE_CHUNK = 256
KEY_CHUNK = 512
LRU_BATCH = 4
VMEM_LIMIT = 56 * 1024 * 1024

F32 = jnp.float32
BF16 = jnp.bfloat16


def _cparams(sem):
    return pltpu.CompilerParams(dimension_semantics=sem, vmem_limit_bytes=VMEM_LIMIT)


def _sigmoid(x):
    return 0.5 * (jnp.tanh(0.5 * x) + 1.0)


def _mod_kernel(cc_ref, w_ref, b_ref, o_ref):
    cc = cc_ref[...]
    s = cc * _sigmoid(cc)
    o_ref[...] = jnp.dot(s, w_ref[...], preferred_element_type=F32,
                         precision=lax.Precision.HIGHEST) + b_ref[...]


def _modulation(cc, w_mod, b_mod):
    depth = w_mod.shape[0]
    rows = cc.shape[0]
    return pl.pallas_call(
        _mod_kernel,
        grid=(depth, 3),
        in_specs=[
            pl.BlockSpec((rows, D_MODEL), lambda l, j: (0, 0)),
            pl.BlockSpec((None, D_MODEL, D_MODEL), lambda l, j: (l, 0, j)),
            pl.BlockSpec((None, 1, D_MODEL), lambda l, j: (l, 0, j)),
        ],
        out_specs=pl.BlockSpec((None, rows, D_MODEL), lambda l, j: (l, 0, j)),
        out_shape=jax.ShapeDtypeStruct((depth, rows, 3 * D_MODEL), F32),
        compiler_params=_cparams(("arbitrary", "arbitrary")),
        name="modulation",
    )(cc, w_mod, b_mod.reshape(depth, 1, 3 * D_MODEL))


def _norm_mod(x, g, mod):
    ms = jnp.mean(x * x, axis=-1, keepdims=True)
    y = x * lax.rsqrt(ms + EPS) * g
    return y * (1.0 + mod[:, D_MODEL:2 * D_MODEL]) + mod[:, :D_MODEL]


def _pair_layout(w):
    lead = w.shape[:-1]
    nh = w.shape[-1] // HEAD_DIM
    return w.reshape(*lead, nh, 2, 2, HEAD_DIM // 4).swapaxes(-3, -2).reshape(*lead, nh * HEAD_DIM)


def _rope(yh, c, s):
    return yh * c + pltpu.roll(yh, HEAD_DIM // 2, 1) * s


def _head_rms_scale(y):
    two = 2 * HEAD_DIM
    rows = lax.broadcasted_iota(jnp.int32, (two, two), 0) // HEAD_DIM
    cols = lax.broadcasted_iota(jnp.int32, (two, two), 1) // HEAD_DIM
    avg = jnp.where(rows == cols, 1.0 / HEAD_DIM, 0.0).astype(BF16)
    sq = (y * y).astype(BF16)
    ms = [jnp.dot(sq[:, p * two:(p + 1) * two], avg, preferred_element_type=F32) for p in range(y.shape[1] // two)]
    return lax.rsqrt(jnp.concatenate(ms, axis=1) + EPS)


def _row_source_specs(X, Xc, tm, nlat):
    if Xc is None:
        return [pl.BlockSpec((tm, D_MODEL), lambda i: (i, 0)), pl.BlockSpec((tm, D_MODEL), lambda i: (0, 0))], (X, X)
    return [pl.BlockSpec((tm, D_MODEL), lambda i: (jnp.minimum(i, nlat - 1), 0)),
            pl.BlockSpec((tm, D_MODEL), lambda i: (jnp.maximum(i - nlat, 0), 0))], (X, Xc)


def _rows(split, x_ref, xc_ref):
    if split is None:
        return x_ref[...]
    return jnp.where(pl.program_id(0) < split, x_ref[...], xc_ref[...])


def _inproj_kernel(specs, split, x_ref, xc_ref, mod_ref, g_ref, w_ref, c_ref, s_ref, qg_ref, kg_ref, *out_refs):
    h = _norm_mod(_rows(split, x_ref, xc_ref), g_ref[...], mod_ref[...]).astype(BF16)
    off = 0
    for (kind, width), o_ref in zip(specs, out_refs):
        y = jnp.dot(h, w_ref[:, off:off + width], preferred_element_type=F32)
        off += width
        if kind == "plain":
            o_ref[...] = y.astype(o_ref.dtype)
        elif kind == "value":
            for hh in range(width // HEAD_DIM):
                o_ref[:, 2 * hh * HEAD_DIM:(2 * hh + 1) * HEAD_DIM] = (
                    y[:, hh * HEAD_DIM:(hh + 1) * HEAD_DIM].astype(o_ref.dtype))
                o_ref[:, (2 * hh + 1) * HEAD_DIM:(2 * hh + 2) * HEAD_DIM] = jnp.ones(
                    (y.shape[0], HEAD_DIM), o_ref.dtype)
        elif kind == "silu":
            o_ref[...] = (y * _sigmoid(y)).astype(o_ref.dtype)
        elif kind == "sigmoid":
            o_ref[...] = _sigmoid(y).astype(o_ref.dtype)
        else:
            is_q = kind in ("rope_q", "normrope_q")
            norm = kind in ("normrope_q", "normrope_k")
            c, s = c_ref[...], s_ref[...]
            scale = _head_rms_scale(y) if norm else None
            for hh in range(width // HEAD_DIM):
                sl = slice(hh * HEAD_DIM, (hh + 1) * HEAD_DIM)
                yh = y[:, sl]
                if norm:
                    yh = yh * scale[:, sl] * (qg_ref[...] if is_q else kg_ref[...])
                yh = _rope(yh, c, s)
                if is_q:
                    yh = yh * Q_SCALE
                o_ref[:, sl] = yh.astype(o_ref.dtype)


def _inproj(X, Xc, mod, g, w, tabs, qg, kg, specs, dtypes, dims, tm, name):
    B, S, L = dims
    n = B * (S + L)
    ts = S // tm
    nlat = B * ts
    wtot = w.shape[1]

    def mod_idx(i):
        return (jnp.where(i < nlat, i // ts, B), 0, 0)

    def tab_idx(i):
        return (jnp.where(i < nlat, i % ts, ts), 0)

    widths = [2 * wd if kind == "value" else wd for kind, wd in specs]
    out_shape = [jax.ShapeDtypeStruct((n, wd), dt) for wd, dt in zip(widths, dtypes)]
    out_specs = [pl.BlockSpec((tm, wd), lambda i: (i, 0)) for wd in widths]
    tab_spec = pl.BlockSpec((tm, HEAD_DIM), tab_idx)
    x_specs, x_args = _row_source_specs(X, Xc, tm, nlat)
    return pl.pallas_call(
        functools.partial(_inproj_kernel, specs, None if Xc is None else nlat),
        grid=(n // tm,),
        in_specs=x_specs + [
            pl.BlockSpec((None, 1, 3 * D_MODEL), mod_idx),
            pl.BlockSpec((1, D_MODEL), lambda i: (0, 0)),
            pl.BlockSpec((D_MODEL, wtot), lambda i: (0, 0)),
            tab_spec, tab_spec,
            pl.BlockSpec((1, HEAD_DIM), lambda i: (0, 0)),
            pl.BlockSpec((1, HEAD_DIM), lambda i: (0, 0)),
        ],
        out_specs=out_specs,
        out_shape=out_shape,
        compiler_params=_cparams(("parallel",)),
        name=name,
    )(*x_args, mod, g, w, tabs[0], tabs[1], qg, kg)


def _lru_kernel(reverse, nchunk, u_ref, up_ref, un_ref, h0_ref, cw_ref, cb_ref, wg_ref, ba_ref, bx_ref, lam_ref,
                *rest):
    y_ref, hout_ref, ext_s, a_s, b_s, h_s = rest[-6:]
    K, T, _ = u_ref.shape
    c = pl.program_id(1)
    j = (nchunk - 1 - c) if reverse else c
    has_prev = j > 0
    has_next = j < nchunk - 1

    lam = lam_ref[...]
    neg = -lam
    softplus = jnp.maximum(neg, 0.0) + jnp.log1p(jnp.exp(-jnp.abs(neg)))
    c1 = (-0.5 * LRU_C * LOG2E) * softplus
    hba = 0.5 * ba_ref[...]
    hbx = 0.5 * bx_ref[...]
    cw = 0.5 * cw_ref[...]
    cb = 0.5 * cb_ref[...]

    for kb in range(K):
        ext_s[kb, 0:8, :] = jnp.where(has_prev, up_ref[kb], 0.0)
        ext_s[kb, 8:8 + T, :] = u_ref[kb]
        ext_s[kb, 8 + T:16 + T, :] = jnp.where(has_next, un_ref[kb], 0.0)
        ext = ext_s[kb]
        hu = cb
        for k in range(CONV_W):
            shift = (CONV_LEFT - k) % (T + 16)
            tap = ext if shift == 0 else pltpu.roll(ext, shift, 0)
            hu = hu + tap[8:8 + T, :] * cw[k:k + 1, :]
        ub = hu.astype(BF16)
        for q in range(D_MODEL // GATE_CHUNK):
            sl = slice(q * GATE_CHUNK, (q + 1) * GATE_CHUNK)
            z = jnp.dot(ub[:, sl], wg_ref[q], preferred_element_type=F32)
            tr = jnp.tanh(z[:, :GATE_CHUNK] + hba[:, sl])
            ti = jnp.tanh(z[:, GATE_CHUNK:] + hbx[:, sl])
            a = jnp.exp2(c1[:, sl] * tr + c1[:, sl])
            x = 1.0 - a * a
            root = jnp.where(x > 0.0, x * lax.rsqrt(x), 0.0)
            b = root * ((ti + 1.0) * hu[:, sl])
            for n in range(GATE_CHUNK // HEAD_DIM):
                lanes = slice(n * HEAD_DIM, (n + 1) * HEAD_DIM)
                sub = q * (GATE_CHUNK // HEAD_DIM) + n
                a_s[kb, pl.ds(sub, T, stride=8), :] = a[:, lanes]
                b_s[kb, pl.ds(sub, T, stride=8), :] = b[:, lanes]

    @pl.when(c == 0)
    def _():
        h_s[...] = h0_ref[...]

    def body(t, hs):
        tt = (T - 1 - t) if reverse else t
        rows = pl.ds(pl.multiple_of(tt * 8, 8), 8)
        out = []
        for kb in range(K):
            h = a_s[kb, rows, :] * hs[kb] + b_s[kb, rows, :]
            b_s[kb, rows, :] = h
            out.append(h)
        return tuple(out)

    hs = lax.fori_loop(0, T, body, tuple(h_s[kb] for kb in range(K)), unroll=8)
    for kb in range(K):
        h_s[kb] = hs[kb]
        for n in range(D_MODEL // HEAD_DIM):
            y_ref[kb, :, n * HEAD_DIM:(n + 1) * HEAD_DIM] = b_s[kb, pl.ds(n, T, stride=8), :].astype(y_ref.dtype)

    @pl.when(c == nchunk - 1)
    def _():
        hout_ref[...] = h_s[...]


def _lru_call(u3, h0, y_prev, weights, K, T, seg0, reverse, name):
    rows = u3.shape[1]
    nchunk = rows // T
    t8 = T // 8
    blk0 = seg0 // K

    def chunk(c):
        return (nchunk - 1 - c) if reverse else c

    main = pl.BlockSpec((K, T, D_MODEL), lambda g, c: (blk0 + g, chunk(c), 0))
    prev = pl.BlockSpec((K, 8, D_MODEL), lambda g, c: (blk0 + g, jnp.maximum(chunk(c) * t8 - 1, 0), 0))
    nxt = pl.BlockSpec((K, 8, D_MODEL),
                       lambda g, c: (blk0 + g, jnp.minimum((chunk(c) + 1) * t8, rows // 8 - 1), 0))
    hspec = pl.BlockSpec((None, K, 8, HEAD_DIM), lambda g, c: (g, 0, 0, 0))
    fixed = lambda shape: pl.BlockSpec(shape, lambda g, c: (0,) * len(shape))
    in_specs = [main, prev, nxt, hspec, fixed((CONV_W, D_MODEL)), fixed((1, D_MODEL)),
                fixed((D_MODEL // GATE_CHUNK, GATE_CHUNK, 2 * GATE_CHUNK)),
                fixed((1, D_MODEL)), fixed((1, D_MODEL)), fixed((1, D_MODEL))]
    args = [u3, u3, u3, h0, *weights]
    aliases = {}
    if y_prev is not None:
        in_specs.append(pl.BlockSpec(memory_space=pl.ANY))
        args.append(y_prev)
        aliases = {len(args) - 1: 0}
    return pl.pallas_call(
        functools.partial(_lru_kernel, reverse, nchunk),
        grid=(h0.shape[0], nchunk),
        in_specs=in_specs,
        out_specs=[main, hspec],
        out_shape=[jax.ShapeDtypeStruct(u3.shape, BF16), jax.ShapeDtypeStruct(h0.shape, F32)],
        input_output_aliases=aliases,
        scratch_shapes=[
            pltpu.VMEM((K, T + 16, D_MODEL), F32),
            pltpu.VMEM((K, T * 8, HEAD_DIM), F32),
            pltpu.VMEM((K, T * 8, HEAD_DIM), F32),
            pltpu.VMEM((K, 8, HEAD_DIM), F32),
        ],
        compiler_params=_cparams(("arbitrary", "arbitrary")),
        name=name,
    )(*args)


def _lru(uA, weights, dims, reverse, name):
    B, S, L = dims
    n = uA.shape[0]
    K = LRU_BATCH
    assert B % K == 0 and n % S == 0 and (B * S // L) % K == 0
    assert D_MODEL == 8 * HEAD_DIM
    h0 = jnp.zeros((B // K, K, 8, HEAD_DIM), F32)
    y, h = _lru_call(uA.reshape(n // L, L, D_MODEL), h0, None, weights, K, L, B * S // L, reverse, name + "_ctx")
    y, _ = _lru_call(uA.reshape(n // S, S, D_MODEL), h, y.reshape(n // S, S, D_MODEL), weights, K, L, 0, reverse,
                     name)
    return y.reshape(n, D_MODEL)


def _gate_weights(wa, wx):
    per = GATE_CHUNK // LRU_BLOCK_W
    nchunks = wa.shape[0] // per
    eye = jnp.eye(per, dtype=wa.dtype)

    def bd(w):
        w4 = w.reshape(nchunks, per, LRU_BLOCK_W, LRU_BLOCK_W)
        return jnp.einsum("jkde,kl->jkdle", w4, eye).reshape(nchunks, GATE_CHUNK, GATE_CHUNK)

    return jnp.concatenate([bd(wa), bd(wx)], axis=-1).astype(BF16)


def _row_shift(mrun):
    mb = mrun.astype(BF16)
    return jnp.broadcast_to(mb.max(axis=-1, keepdims=True), mb.shape).astype(F32)


def _shifted_exp2(s, shift):
    blocks = [jnp.exp2(s[:, c * HEAD_DIM:(c + 1) * HEAD_DIM] - shift) for c in range(s.shape[1] // HEAD_DIM)]
    return jnp.concatenate(blocks, axis=1).astype(BF16)


def _lane_fold(x, acc, op):
    for c in range(x.shape[1] // HEAD_DIM):
        blk = x[:, c * HEAD_DIM:(c + 1) * HEAD_DIM]
        acc = blk if acc is None else op(acc, blk)
    return acc


def _softmax_pv(scores, values, sink):
    mrun = None
    for s in scores:
        mrun = _lane_fold(s, mrun, jnp.maximum)
    m = mrun.max(axis=-1, keepdims=True)
    if sink is not None:
        m = jnp.maximum(m, sink)
    o = None
    for s, v in zip(scores, values):
        d = jnp.dot(jnp.exp2(s - m).astype(BF16), v, preferred_element_type=F32)
        o = d if o is None else o + d
    l = o[:, HEAD_DIM:]
    if sink is not None:
        l = l + jnp.exp2(sink - m)
    return o[:, :HEAD_DIM] * (1.0 / l)


def _qk(q, k):
    return lax.dot_general(q, k, (((1,), (1,)), ((), ())), preferred_element_type=F32)


def _stack_heads(q_ref, g):
    return jnp.concatenate(
        [q_ref[:, (g * GROUP + h) * HEAD_DIM:(g * GROUP + h + 1) * HEAD_DIM] for h in range(GROUP)], axis=0)


def _sink_col(sink_ref, g, tq):
    return jnp.concatenate(
        [jnp.full((tq, 1), sink_ref[g * GROUP + h] * LOG2E, F32) for h in range(GROUP)], axis=0)


def _sink_lanes(sink_ref, g, tq):
    return jnp.concatenate(
        [jnp.full((tq, HEAD_DIM), sink_ref[g * GROUP + h] * LOG2E, F32) for h in range(GROUP)], axis=0)


def _store_heads(o_ref, gate_ref, o, g, tq):
    for h in range(GROUP):
        sl = slice((g * GROUP + h) * HEAD_DIM, (g * GROUP + h + 1) * HEAD_DIM)
        o_ref[:, sl] = (o[h * tq:(h + 1) * tq] * gate_ref[:, sl].astype(F32)).astype(o_ref.dtype)


def _kv_slices(g):
    return slice(g * HEAD_DIM, (g + 1) * HEAD_DIM), slice(2 * g * HEAD_DIM, 2 * (g + 1) * HEAD_DIM)


def _window_kernel(nlat, S, sink_ref, q_ref, kl_ref, vl_ref, kc_ref, vc_ref, gate_ref, o_ref, s_a, s_b, m_a, m_b):
    tq = q_ref.shape[0]
    L = kc_ref.shape[0]
    span = tq + 2 * WINDOW
    i = pl.program_id(1)

    def band_start(tile):
        return pl.multiple_of(jnp.clip(tile * tq - WINDOW, 0, S - span), HEAD_DIM)

    def step(write, read):
        if write is not None:
            start = band_start(i)
            qpos = i * tq + lax.broadcasted_iota(jnp.int32, (tq, span), 0)
            kpos = start + lax.broadcasted_iota(jnp.int32, (tq, span), 1)
            bias = jnp.where(jnp.abs(kpos - qpos) <= WINDOW, 0.0, NEG_INF)
            bias = jnp.concatenate([bias] * GROUP, axis=0)
            for g in range(N_KV):
                cs, _ = _kv_slices(g)
                qs = _stack_heads(q_ref, g)
                s_ctx = _qk(qs, kc_ref[:, cs])
                s_lat = _qk(qs, kl_ref[pl.ds(start, span), cs]) + bias
                write[0][g, :, :L] = s_ctx
                write[0][g, :, L:] = s_lat
                mrun = _lane_fold(s_lat, _lane_fold(s_ctx, None, jnp.maximum), jnp.maximum)
                write[1][g] = jnp.maximum(_row_shift(mrun), _sink_lanes(sink_ref, g, tq))
        if read is not None:
            start = band_start(i - 1)
            for g in range(N_KV):
                _, vs = _kv_slices(g)
                m = read[1][g]
                o = jnp.dot(_shifted_exp2(read[0][g, :, :L], m), vc_ref[:, vs], preferred_element_type=F32)
                o = o + jnp.dot(_shifted_exp2(read[0][g, :, L:], m), vl_ref[pl.ds(start, span), vs],
                                preferred_element_type=F32)
                l = o[:, HEAD_DIM:] + jnp.exp2(_sink_lanes(sink_ref, g, tq) - m)
                _store_heads(o_ref, gate_ref, o[:, :HEAD_DIM] * (1.0 / l), g, tq)

    a, b = (s_a, m_a), (s_b, m_b)
    odd = i % 2 == 1
    steady = jnp.logical_and(i > 0, i < nlat)

    @pl.when(i == 0)
    def _():
        step(a, None)

    @pl.when(jnp.logical_and(steady, odd))
    def _():
        step(b, a)

    @pl.when(jnp.logical_and(steady, jnp.logical_not(odd)))
    def _():
        step(a, b)

    @pl.when(i == nlat)
    def _():
        step(None, b)


def _window_attention(q, k, v, gate, sink, dims, tq, rows):
    B, S, L = dims
    nlat = S // tq
    assert nlat % 2 == 0
    M = GROUP * tq
    keys = L + tq + 2 * WINDOW
    lat_spec = lambda wd: pl.BlockSpec((S, wd), lambda b, i: (b, 0))
    ctx_spec = lambda wd: pl.BlockSpec((L, wd), lambda b, i: (B * S // L + b, 0))
    return pl.pallas_call(
        functools.partial(_window_kernel, nlat, S),
        grid=(B, nlat + 1),
        in_specs=[pl.BlockSpec(memory_space=pltpu.SMEM),
                  pl.BlockSpec((tq, D_MODEL), lambda b, i: (b * nlat + jnp.minimum(i, nlat - 1), 0)),
                  lat_spec(KV_W), lat_spec(2 * KV_W), ctx_spec(KV_W), ctx_spec(2 * KV_W),
                  pl.BlockSpec((tq, D_MODEL), lambda b, i: (b * nlat + jnp.maximum(i - 1, 0), 0))],
        out_specs=pl.BlockSpec((tq, D_MODEL), lambda b, i: (b * nlat + jnp.maximum(i - 1, 0), 0)),
        out_shape=jax.ShapeDtypeStruct((rows, D_MODEL), BF16),
        scratch_shapes=[pltpu.VMEM((N_KV, M, keys), F32)] * 2 + [pltpu.VMEM((N_KV, M, HEAD_DIM), F32)] * 2,
        compiler_params=_cparams(("arbitrary", "arbitrary")),
        name="attn_window",
    )(sink, q, k, v, k, v, gate)


def _dense_kernel(nlat, q_ref, kl_ref, vl_ref, kc_ref, vc_ref, gate_ref, o_ref, s_a, s_b, m_a, m_b):
    tq = q_ref.shape[0]
    L = kc_ref.shape[0]
    S = kl_ref.shape[0]
    i = pl.program_id(1)
    bounds = [(0, L)] + [(L + j0, L + j0 + KEY_CHUNK) for j0 in range(0, S, KEY_CHUNK)]

    def step(write, read):
        for g in range(N_KV):
            if write is None:
                break
            cs, _ = _kv_slices(g)
            qs = _stack_heads(q_ref, g)
            mrun = None
            for lo, hi in bounds:
                k = kc_ref[:, cs] if lo == 0 else kl_ref[lo - L:hi - L, cs]
                s = _qk(qs, k)
                write[0][g, :, lo:hi] = s
                mrun = _lane_fold(s, mrun, jnp.maximum)
            write[1][g] = _row_shift(mrun)
        for g in range(N_KV):
            if read is None:
                break
            _, vs = _kv_slices(g)
            m = read[1][g]
            o = None
            for lo, hi in bounds:
                v = vc_ref[:, vs] if lo == 0 else vl_ref[lo - L:hi - L, vs]
                d = jnp.dot(_shifted_exp2(read[0][g, :, lo:hi], m), v, preferred_element_type=F32)
                o = d if o is None else o + d
            _store_heads(o_ref, gate_ref, o[:, :HEAD_DIM] * (1.0 / o[:, HEAD_DIM:]), g, tq)

    a, b = (s_a, m_a), (s_b, m_b)
    odd = i % 2 == 1
    steady = jnp.logical_and(i > 0, i < nlat)

    @pl.when(i == 0)
    def _():
        step(a, None)

    @pl.when(jnp.logical_and(steady, odd))
    def _():
        step(b, a)

    @pl.when(jnp.logical_and(steady, jnp.logical_not(odd)))
    def _():
        step(a, b)

    @pl.when(i == nlat)
    def _():
        step(None, b)


def _dense_attention(q, k, v, gate, dims, tq, rows):
    B, S, L = dims
    nlat = S // tq
    assert nlat % 2 == 0
    one = pl.Buffered(1)
    return pl.pallas_call(
        functools.partial(_dense_kernel, nlat),
        grid=(B, nlat + 1),
        in_specs=[
            pl.BlockSpec((tq, D_MODEL), lambda b, i: (b * nlat + jnp.minimum(i, nlat - 1), 0)),
            pl.BlockSpec((S, KV_W), lambda b, i: (b, 0)),
            pl.BlockSpec((S, 2 * KV_W), lambda b, i: (b, 0)),
            pl.BlockSpec((L, KV_W), lambda b, i: (B * S // L + b, 0), pipeline_mode=one),
            pl.BlockSpec((L, 2 * KV_W), lambda b, i: (B * S // L + b, 0), pipeline_mode=one),
            pl.BlockSpec((tq, D_MODEL), lambda b, i: (b * nlat + jnp.maximum(i - 1, 0), 0)),
        ],
        out_specs=pl.BlockSpec((tq, D_MODEL), lambda b, i: (b * nlat + jnp.maximum(i - 1, 0), 0)),
        out_shape=jax.ShapeDtypeStruct((rows, D_MODEL), BF16),
        scratch_shapes=[pltpu.VMEM((N_KV, GROUP * tq, L + S), F32)] * 2
        + [pltpu.VMEM((N_KV, GROUP * tq, HEAD_DIM), F32)] * 2,
        compiler_params=_cparams(("arbitrary", "arbitrary")),
        name="attn_dense",
    )(q, k, v, k, v, gate)


def _ctx_kernel(sink_ref, qb_ref, kb_ref, vb_ref, gb_ref, qc_ref, kc_ref, vc_ref, gc_ref, yb_any, yc_any,
                ob_ref, oc_ref):
    del yb_any, yc_any
    tq = qb_ref.shape[0]
    for g in range(N_KV):
        cs, vs = _kv_slices(g)
        qs = _stack_heads(qb_ref, g)
        ob = _softmax_pv([_qk(qs, kb_ref[:, cs])], [vb_ref[:, vs]], _sink_col(sink_ref, g, tq))
        _store_heads(ob_ref, gb_ref, ob, g, tq)
        qs = _stack_heads(qc_ref, g)
        oc = _softmax_pv([_qk(qs, kc_ref[:, cs])], [vc_ref[:, vs]], None)
        _store_heads(oc_ref, gc_ref, oc, g, tq)


def _ctx_attention(sink, qB, kB, vB, gB, qC, kC, vC, gC, yB, yC, dims):
    B, S, L = dims
    c0 = B * S // L
    rows = lambda wd: pl.BlockSpec((L, wd), lambda b: (c0 + b, 0))
    anyspec = pl.BlockSpec(memory_space=pl.ANY)
    return pl.pallas_call(
        _ctx_kernel,
        grid=(B,),
        in_specs=[pl.BlockSpec(memory_space=pltpu.SMEM),
                  rows(D_MODEL), rows(KV_W), rows(2 * KV_W), rows(D_MODEL),
                  rows(D_MODEL), rows(KV_W), rows(2 * KV_W), rows(D_MODEL),
                  anyspec, anyspec],
        out_specs=[rows(D_MODEL), rows(D_MODEL)],
        out_shape=[jax.ShapeDtypeStruct(yB.shape, BF16), jax.ShapeDtypeStruct(yC.shape, BF16)],
        input_output_aliases={9: 0, 10: 1},
        compiler_params=_cparams(("arbitrary",)),
        name="attn_ctx",
    )(sink, qB, kB, vB, gB, qC, kC, vC, gC, yB, yC)


def _merge_kernel(final, split, x_ref, xc_ref, mod_ref, yf_ref, yr_ref, ga_ref, yb_ref, yc_ref, m_ref,
                  wb_ref, wo_ref, fg_ref, o_ref):
    ya = (yf_ref[...].astype(F32) + yr_ref[...].astype(F32)) * ga_ref[...].astype(F32)
    ts = (ya.astype(BF16), yb_ref[...], yc_ref[...])
    acc = 0.0
    for n in range(3):
        br = jnp.dot(ts[n], wb_ref[n], preferred_element_type=F32)
        acc = acc + m_ref[:, n * D_MODEL:(n + 1) * D_MODEL].astype(F32) * br
    y = jnp.dot(acc.astype(BF16), wo_ref[...], preferred_element_type=F32)
    xn = _rows(split, x_ref, xc_ref) + mod_ref[:, 2 * D_MODEL:] * y
    if final:
        xn = xn * lax.rsqrt(jnp.mean(xn * xn, axis=-1, keepdims=True) + EPS) * fg_ref[...]
    o_ref[...] = xn


def _merge(X, Xc, mod, yf, yr, ga, yb, yc, m, wb, wo, fg, dims, tm, final):
    B, S, L = dims
    n = B * S if final else B * (S + L)
    ts = S // tm
    nlat = B * ts
    x_specs, x_args = _row_source_specs(X, Xc, tm, nlat)

    def mod_idx(i):
        return (jnp.where(i < nlat, i // ts, B), 0, 0)

    rows = lambda wd: pl.BlockSpec((tm, wd), lambda i: (i, 0))
    return pl.pallas_call(
        functools.partial(_merge_kernel, final, None if Xc is None else nlat),
        grid=(n // tm,),
        in_specs=x_specs + [
            pl.BlockSpec((None, 1, 3 * D_MODEL), mod_idx),
            rows(D_MODEL), rows(D_MODEL), rows(D_MODEL), rows(D_MODEL), rows(D_MODEL), rows(3 * D_MODEL),
            pl.BlockSpec((3, D_MODEL, D_MODEL), lambda i: (0, 0, 0), pipeline_mode=pl.Buffered(1)),
            pl.BlockSpec((D_MODEL, D_MODEL), lambda i: (0, 0), pipeline_mode=pl.Buffered(1)),
            pl.BlockSpec((1, D_MODEL), lambda i: (0, 0)),
        ],
        out_specs=rows(D_MODEL),
        out_shape=jax.ShapeDtypeStruct((n, D_MODEL), F32),
        compiler_params=_cparams(("parallel",)),
        name="merge_final" if final else "merge",
    )(*x_args, mod, yf, yr, ga, yb, yc, m, wb, wo, fg)


def _rope_tables(S, tm):
    t = jnp.arange(S)
    P = HEAD_DIM // 4
    inv = ROPE_THETA ** (-jnp.arange(P, dtype=F32) / P)
    ang = jnp.stack([(t // GRID_W)[:, None] * inv, (t % GRID_W)[:, None] * inv], axis=1)
    cos = jnp.cos(ang).reshape(S, HEAD_DIM // 2)
    sin = jnp.sin(ang).reshape(S, HEAD_DIM // 2)
    c = jnp.concatenate([cos, cos], axis=1)
    s = jnp.concatenate([-sin, sin], axis=1)
    ident = jnp.zeros((tm, HEAD_DIM), F32)
    return jnp.concatenate([c, ident + 1.0]), jnp.concatenate([s, ident])


def kernel(x, c, ctx, c_ctx, norm_g, w_mod, b_mod, w_in, conv_w, conv_b, lru_wa, lru_ba, lru_wx, lru_bx,
           lru_lambda, attn_sink, q_norm_g, k_norm_g, w_branch, w_out, final_g):
    B, S, _ = x.shape
    L = ctx.shape[1]
    depth = w_in.shape[0]
    dims = (B, S, L)
    assert S % L == 0 and L % 128 == 0 and S % GRID_W == 0 and S >= 256 + 2 * WINDOW
    tm = math.gcd(1024, math.gcd(S, B * L))
    tm_merge = math.gcd(512, tm)
    tq_w = 256
    tq_g = 128

    X, Xc = x.reshape(B * S, D_MODEL), ctx.reshape(B * L, D_MODEL)
    rows = -(-(B + 1) // 8) * 8
    cc = jnp.concatenate([c, c_ctx[None], jnp.zeros((rows - B - 1, D_MODEL), F32)], axis=0)
    mod_all = _modulation(cc, w_mod, b_mod).reshape(depth, rows, 1, 3 * D_MODEL)
    tabs = _rope_tables(S, tm)
    w_in16 = w_in.astype(BF16)
    cuts = (2 * D_MODEL, 3 * D_MODEL, 3 * D_MODEL + KV_W, 4 * D_MODEL + 2 * KV_W, 5 * D_MODEL + 2 * KV_W,
            5 * D_MODEL + 3 * KV_W)
    w_in16 = jnp.concatenate(
        [w_in16[..., :cuts[0]], _pair_layout(w_in16[..., cuts[0]:cuts[1]]), _pair_layout(w_in16[..., cuts[1]:cuts[2]]),
         w_in16[..., cuts[2]:cuts[3]], _pair_layout(w_in16[..., cuts[3]:cuts[4]]),
         _pair_layout(w_in16[..., cuts[4]:cuts[5]]), w_in16[..., cuts[5]:]], axis=-1)
    wb16 = w_branch.astype(BF16)
    wo16 = w_out.astype(BF16)
    row = lambda a: a.reshape(1, -1)

    a_specs = (("plain", D_MODEL), ("silu", D_MODEL))
    b_specs = (("rope_q", D_MODEL), ("rope_k", KV_W), ("value", KV_W), ("silu", D_MODEL))
    c_specs = (("normrope_q", D_MODEL), ("normrope_k", KV_W), ("value", KV_W), ("silu", D_MODEL))
    m_specs = (("sigmoid", 3 * D_MODEL),)
    a_w = 2 * D_MODEL
    b_w = 2 * D_MODEL + 2 * KV_W

    out = None
    for l in range(depth):
        last = l == depth - 1
        mod = mod_all[l]
        g = row(norm_g[l])
        qg, kg = _pair_layout(row(q_norm_g[l])), _pair_layout(row(k_norm_g[l]))
        w = w_in16[l]
        proj = functools.partial(_inproj, X, Xc, mod, g, tabs=tabs, qg=qg, kg=kg, dims=dims, tm=tm)
        uA, gA, m = proj(w=jnp.concatenate([w[:, :a_w], w[:, a_w + 2 * b_w:]], axis=1), specs=a_specs + m_specs,
                         dtypes=(F32, BF16, BF16), tm=math.gcd(512, tm), name="inproj_am")
        qB, kB, vB, gB = proj(w=w[:, a_w:a_w + b_w], specs=b_specs, dtypes=(BF16,) * 4, name="inproj_b")
        qC, kC, vC, gC = proj(w=w[:, a_w + b_w:a_w + 2 * b_w], specs=c_specs, dtypes=(BF16,) * 4,
                              name="inproj_c")

        ys = []
        for d, rev in enumerate((False, True)):
            wg = _gate_weights(lru_wa[l, d], lru_wx[l, d])
            weights = (conv_w[l], row(conv_b[l]), wg, row(lru_ba[l, d]), row(lru_bx[l, d]), row(lru_lambda[l, d]))
            ys.append(_lru(uA, weights, dims, rev, "lru_rev" if rev else "lru_fwd"))
        rows_out = B * S if last else B * (S + L)
        yB = _window_attention(qB, kB, vB, gB, attn_sink[l], dims, tq_w, rows_out)
        yC = _dense_attention(qC, kC, vC, gC, dims, tq_g, rows_out)
        if not last:
            yB, yC = _ctx_attention(attn_sink[l], qB, kB, vB, gB, qC, kC, vC, gC, yB, yC, dims)
        res = _merge(X, Xc, mod, ys[0], ys[1], gA, yB, yC, m, wb16[l], wo16[l], row(final_g), dims, tm_merge,
                     last)
        if last:
            out = res.reshape(B, S, D_MODEL)
        else:
            X, Xc = res, None
    return out
```
